```python
import jax
import jax.numpy as jnp
from jax import lax
import numpy as np

D_MODEL = 2048
BATCH = 2
SEQ = 8192
DEPTH = 4

N_META = 16
CHUNK = 64
NORM_EPS = 1e-6
NEG_BIG = -1e30
F_FLOOR = 1e-12
BRANCH_WIDTH = 1024
N_BRANCH = 3

A_HEADS = 8
A_DK = 128
A_DV = BRANCH_WIDTH // A_HEADS
B_HEADS = 4
B_DQK = 128
B_DV = BRANCH_WIDTH // B_HEADS
B_CONV = 4
C_HEADS = 16
C_DH = BRANCH_WIDTH // C_HEADS
C_DECAY_RANK = 64
C_ICLR_RANK = 64
C_LN_EPS = 64e-5

SPLITS = (
    A_HEADS * A_DK, A_HEADS * A_DK, BRANCH_WIDTH, BRANCH_WIDTH,
    B_HEADS * B_DQK, B_HEADS * B_DQK, BRANCH_WIDTH, BRANCH_WIDTH,
    B_HEADS, B_HEADS, BRANCH_WIDTH,
    BRANCH_WIDTH, BRANCH_WIDTH, BRANCH_WIDTH, C_DECAY_RANK, C_ICLR_RANK,
    BRANCH_WIDTH,
    D_MODEL, D_MODEL, D_MODEL,
)
N_IN = sum(SPLITS)
C_SHIFT_WIDTH = 3 * BRANCH_WIDTH + C_DECAY_RANK + C_ICLR_RANK

kernel_name = 'hybrid_hgrn2_mlstm_rwkv7_gated'


def rmsnorm(x, g):
    xf = x.astype(jnp.float32)
    y = xf * lax.rsqrt(jnp.mean(xf * xf, axis=-1, keepdims=True) + NORM_EPS)
    return (y * g.astype(jnp.float32)).astype(x.dtype)


def split_heads(a, n_heads):
    return a.reshape(a.shape[:-1] + (n_heads, a.shape[-1] // n_heads))


def head_rmsnorm(y, g):
    y = y * lax.rsqrt(jnp.mean(y * y, axis=-1, keepdims=True) + NORM_EPS)
    return y.reshape(y.shape[:2] + (-1,)) * g


def head_layernorm(y, g, eps):
    yc = y - jnp.mean(y, axis=-1, keepdims=True)
    y = yc * lax.rsqrt(jnp.mean(yc * yc, axis=-1, keepdims=True) + eps)
    return y.reshape(y.shape[:2] + (-1,)) * g


def split_cols(p):
    points = [int(v) for v in np.cumsum(SPLITS)[:-1]]
    return jnp.split(p, points, axis=-1)


def causal_conv(x, w):
    k = w.shape[0]
    return lax.conv_general_dilated(
        x, w[:, None, :].astype(x.dtype), window_strides=(1,), padding=[(k - 1, 0)],
        dimension_numbers=('NWC', 'WIO', 'NWC'), feature_group_count=x.shape[-1])


def causal_mask(length, strict=False):
    return jnp.tril(jnp.ones((length, length), dtype=bool), k=-1 if strict else 0)


def pair_decay(cum_t, cum_s, mask):
    m = mask[:, :, None]
    diff = cum_t[:, :, :, None, :] - cum_s[:, :, None, :, :]
    return jnp.where(m, jnp.exp(jnp.where(m, diff, 0.0)), 0.0)


def to_chunks(a):
    b, s, h, d = a.shape
    return a.reshape(b, s // CHUNK, CHUNK, h, d).transpose(1, 0, 3, 2, 4)


def run_chunked(step, state, seqs):
    meta = tuple(a[:, :N_META].transpose(0, 2, 1, 3) for a in seqs)
    real = tuple(to_chunks(a[:, N_META:]) for a in seqs)
    state, y_meta = step(state, meta)
    _, y_real = lax.scan(step, state, real)
    n_chunks, bsz, n_heads, length, dv = y_real.shape
    y_real = y_real.transpose(1, 0, 3, 2, 4).reshape(bsz, n_chunks * length, n_heads, dv)
    return jnp.concatenate([y_meta.transpose(0, 2, 1, 3), y_real], axis=1)


def hgrn2_chunk(s_mat, inp):
    q, k, log_f, v = inp
    length = q.shape[2]
    cg = jnp.cumsum(log_f, axis=2)
    att = jnp.einsum('bhtd,bhsd,bhtsd->bhts', q, k, pair_decay(cg, cg, causal_mask(length)))
    o = jnp.einsum('bhts,bhsv->bhtv', att, v) + jnp.einsum('bhtd,bhdv->bhtv', q * jnp.exp(cg), s_mat)
    tail = jnp.exp(cg[:, :, -1:] - cg)
    s_mat = jnp.exp(cg[:, :, -1])[..., None] * s_mat + jnp.einsum('bhsd,bhsv->bhdv', k * tail, v)
    return s_mat, o


def hgrn2_mixer(q_raw, f_raw, i_raw, z, lb, norm_g):
    f32 = jnp.float32
    f_raw = f_raw.astype(f32)
    q = split_heads(jax.nn.silu(q_raw.astype(f32)), A_HEADS) * A_DK ** -0.5
    k = (1.0 - lb) * jax.nn.sigmoid(-f_raw)
    log_f = jnp.log(jnp.maximum(lb + (1.0 - lb) * jax.nn.sigmoid(f_raw), F_FLOOR))
    v = split_heads(i_raw.astype(f32), A_HEADS)
    s0 = jnp.zeros((q.shape[0], A_HEADS, A_DK, A_DV), f32)
    o = run_chunked(hgrn2_chunk, s0, (q, split_heads(k, A_HEADS), split_heads(log_f, A_HEADS), v))
    return head_rmsnorm(o, norm_g) * jax.nn.silu(z.astype(f32))


def mlstm_chunk(state, inp):
    c_mat, n_vec, m_prev = state
    q, k, v, ig, lf = inp
    ig = ig[..., 0]
    length = q.shape[2]
    b = jnp.cumsum(lf[..., 0], axis=-1)
    log_w = b[:, :, :, None] - b[:, :, None, :] + ig[:, :, None, :]
    log_w = jnp.where(causal_mask(length), log_w, NEG_BIG)
    log_inter = b + m_prev[..., None]
    m_t = jnp.maximum(log_inter, jnp.max(log_w, axis=-1))
    scores = jnp.einsum('bhtd,bhsd->bhts', q, k) * jnp.exp(log_w - m_t[..., None])
    inter = jnp.exp(log_inter - m_t)
    num = jnp.einsum('bhts,bhsv->bhtv', scores, v) + inter[..., None] * jnp.einsum('bhtd,bhdv->bhtv', q, c_mat)
    den = jnp.sum(scores, axis=-1) + inter * jnp.einsum('bhtd,bhd->bht', q, n_vec)
    h = num / jnp.maximum(jnp.abs(den), jnp.exp(-m_t))[..., None]
    b_end = b[:, :, -1]
    log_s = b_end[..., None] - b + ig
    m_new = jnp.maximum(b_end + m_prev, jnp.max(log_s, axis=-1))
    w_s = jnp.exp(log_s - m_new[..., None])
    carry = jnp.exp(b_end + m_prev - m_new)
    c_mat = carry[..., None, None] * c_mat + jnp.einsum('bhs,bhsd,bhsv->bhdv', w_s, k, v)
    n_vec = carry[..., None] * n_vec + jnp.einsum('bhs,bhsd->bhd', w_s, k)
    return (c_mat, n_vec, m_new), h


def mlstm_mixer(q_raw, k_raw, v_raw, o_raw, ig_raw, fg_raw, z, conv_w, ig_b, fg_b, norm_g):
    f32 = jnp.float32
    qk = jax.nn.silu(causal_conv(jnp.concatenate([q_raw, k_raw], axis=-1).astype(f32), conv_w))
    q, k = jnp.split(qk, 2, axis=-1)
    q = split_heads(q, B_HEADS) * B_DQK ** -0.5
    k = split_heads(k, B_HEADS)
    v = split_heads(v_raw.astype(f32), B_HEADS)
    ig = (ig_raw.astype(f32) + ig_b)[..., None]
    lf = jax.nn.log_sigmoid(fg_raw.astype(f32) + fg_b)[..., None]
    bsz = q.shape[0]
    state = (jnp.zeros((bsz, B_HEADS, B_DQK, B_DV), f32), jnp.zeros((bsz, B_HEADS, B_DQK), f32),
             jnp.zeros((bsz, B_HEADS), f32))
    h = run_chunked(mlstm_chunk, state, (q, k, v, ig, lf))
    return (head_layernorm(h, norm_g, NORM_EPS) * jax.nn.sigmoid(o_raw.astype(f32))
            * jax.nn.silu(z.astype(f32)))


def rwkv7_chunk(s_mat, inp):
    r, k, v, lw, a_, b = inp
    length = r.shape[2]
    cw = jnp.cumsum(lw, axis=2)
    cw_prev = cw - lw
    d_strict = pair_decay(cw_prev, cw, causal_mask(length, strict=True))
    d_incl = pair_decay(cw, cw, causal_mask(length))
    l_ab = jnp.einsum('bhtd,bhsd,bhtsd->bhts', a_, b, d_strict)
    l_ak = jnp.einsum('bhtd,bhsd,bhtsd->bhts', a_, k, d_strict)
    rhs = (jnp.einsum('bhtd,bhdv->bhtv', a_ * jnp.exp(cw_prev), s_mat)
           + jnp.einsum('bhts,bhsv->bhtv', l_ak, v))
    u = lax.linalg.triangular_solve(jnp.eye(length, dtype=rhs.dtype) - l_ab, rhs,
                                    left_side=True, lower=True, unit_diagonal=True)
    r_b = jnp.einsum('bhtd,bhsd,bhtsd->bhts', r, b, d_incl)
    r_k = jnp.einsum('bhtd,bhsd,bhtsd->bhts', r, k, d_incl)
    y = (jnp.einsum('bhtd,bhdv->bhtv', r * jnp.exp(cw), s_mat)
         + jnp.einsum('bhts,bhsv->bhtv', r_b, u) + jnp.einsum('bhts,bhsv->bhtv', r_k, v))
    tail = jnp.exp(cw[:, :, -1:] - cw)
    s_mat = (jnp.exp(cw[:, :, -1])[..., None] * s_mat
             + jnp.einsum('bhsd,bhsv->bhdv', b * tail, u) + jnp.einsum('bhsd,bhsv->bhdv', k * tail, v))
    return s_mat, y


def rwkv7_mixer(c_r, c_k, c_v, c_wd, c_ad, z, mu, w0, w_up, a0, a_up, k_k, k_a, r_k, ln_g, ln_b):
    f32 = jnp.float32
    m = jnp.concatenate([c_r, c_k, c_v, c_wd, c_ad], axis=-1).astype(f32)
    prev = jnp.pad(m, ((0, 0), (1, 0), (0, 0)))[:, :-1]
    m = m + (prev - m) * mu
    w = BRANCH_WIDTH
    r, k, v, wd, ad = jnp.split(m, [w, 2 * w, 3 * w, 3 * w + C_DECAY_RANK], axis=-1)
    w_log = -jax.nn.softplus(-(w0 + jnp.tanh(wd) @ w_up)) - 0.5
    lw = -jnp.exp(w_log)
    a = jax.nn.sigmoid(a0 + ad @ a_up)
    kk = split_heads(k * k_k, C_HEADS)
    kk = kk / jnp.maximum(jnp.sqrt(jnp.sum(kk * kk, axis=-1, keepdims=True)), 1e-12)
    k = split_heads(k * (1.0 + (a - 1.0) * k_a), C_HEADS)
    r = split_heads(r, C_HEADS)
    v = split_heads(v, C_HEADS)
    s0 = jnp.zeros((r.shape[0], C_HEADS, C_DH, C_DH), f32)
    y = run_chunked(rwkv7_chunk, s0, (r, k, v, split_heads(lw, C_HEADS), -kk, kk * split_heads(a, C_HEADS)))
    y = head_layernorm(y, ln_g, C_LN_EPS) + ln_b
    bonus = jnp.sum(r * k * r_k.reshape(C_HEADS, C_DH), axis=-1, keepdims=True) * v
    return (y + bonus.reshape(y.shape)) * jax.nn.silu(z.astype(f32))


def setup_inputs(seed: int = 0) -> dict:
    key = jax.random.key(seed)
    ks = jax.random.split(key, 24)
    f32 = jnp.float32

    def nrm(k, shape, scale):
        return jax.random.normal(k, shape, f32) * scale

    return {
        'x': nrm(ks[0], (BATCH, SEQ, D_MODEL), 1.0),
        'meta_tokens': nrm(ks[1], (N_META, D_MODEL), 1.0),
        'norm_g': 1.0 + nrm(ks[2], (DEPTH, D_MODEL), 0.02),
        'w_in': nrm(ks[3], (DEPTH, D_MODEL, N_IN), D_MODEL ** -0.5),
        'hgrn_lb_logits': nrm(ks[4], (DEPTH, A_HEADS * A_DK), 0.5),
        'hgrn_norm_g': 1.0 + nrm(ks[5], (DEPTH, BRANCH_WIDTH), 0.02),
        'mlstm_conv': nrm(ks[6], (DEPTH, B_CONV, 2 * B_HEADS * B_DQK), B_CONV ** -0.5),
        'mlstm_ig_b': nrm(ks[7], (DEPTH, B_HEADS), 0.1),
        'mlstm_fg_b': jnp.linspace(3.0, 6.0, B_HEADS, dtype=f32)[None] + nrm(ks[8], (DEPTH, B_HEADS), 0.1),
        'mlstm_norm_g': 1.0 + nrm(ks[9], (DEPTH, BRANCH_WIDTH), 0.02),
        'rwkv_mu': jax.random.uniform(ks[10], (DEPTH, C_SHIFT_WIDTH), f32),
        'rwkv_w0': jnp.linspace(-6.0, -1.0, BRANCH_WIDTH, dtype=f32)[None] + nrm(ks[11], (DEPTH, BRANCH_WIDTH), 0.1),
        'rwkv_w_up': nrm(ks[12], (DEPTH, C_DECAY_RANK, BRANCH_WIDTH), 0.5 * C_DECAY_RANK ** -0.5),
        'rwkv_a0': nrm(ks[13], (DEPTH, BRANCH_WIDTH), 0.1),
        'rwkv_a_up': nrm(ks[14], (DEPTH, C_ICLR_RANK, BRANCH_WIDTH), 0.5 * C_ICLR_RANK ** -0.5),
        'rwkv_k_k': 0.85 + nrm(ks[15], (DEPTH, BRANCH_WIDTH), 0.02),
        'rwkv_k_a': 1.0 + nrm(ks[16], (DEPTH, BRANCH_WIDTH), 0.02),
        'rwkv_r_k': nrm(ks[17], (DEPTH, BRANCH_WIDTH), 0.1),
        'rwkv_ln_g': 1.0 + nrm(ks[18], (DEPTH, BRANCH_WIDTH), 0.02),
        'rwkv_ln_b': nrm(ks[19], (DEPTH, BRANCH_WIDTH), 0.02),
        'w_br': nrm(ks[20], (DEPTH, N_BRANCH, BRANCH_WIDTH, D_MODEL), BRANCH_WIDTH ** -0.5),
        'w_out': nrm(ks[21], (DEPTH, D_MODEL, D_MODEL), D_MODEL ** -0.5),
        'final_norm_g': 1.0 + nrm(ks[22], (D_MODEL,), 0.02),
    }


def reference(x, meta_tokens, norm_g, w_in, hgrn_lb_logits, hgrn_norm_g, mlstm_conv, mlstm_ig_b,
              mlstm_fg_b, mlstm_norm_g, rwkv_mu, rwkv_w0, rwkv_w_up, rwkv_a0, rwkv_a_up, rwkv_k_k,
              rwkv_k_a, rwkv_r_k, rwkv_ln_g, rwkv_ln_b, w_br, w_out, final_norm_g):
    dt = x.dtype
    bsz = x.shape[0]
    meta = jnp.broadcast_to(meta_tokens[None].astype(dt), (bsz, N_META, D_MODEL))
    h = jnp.concatenate([meta, x], axis=1)
    p = jax.nn.softmax(hgrn_lb_logits.astype(jnp.float32), axis=0)
    lower_bounds = jnp.cumsum(p, axis=0) - p[0]
    for l in range(DEPTH):
        xn = rmsnorm(h, norm_g[l])
        (a_q, a_f, a_i, a_z, b_q, b_k, b_v, b_o, b_ig, b_fg, b_z,
         c_r, c_k, c_v, c_wd, c_ad, c_z, g_a, g_b, g_c) = split_cols(xn @ w_in[l])
        y_a = hgrn2_mixer(a_q, a_f, a_i, a_z, lower_bounds[l], hgrn_norm_g[l]).astype(dt)
        y_b = mlstm_mixer(b_q, b_k, b_v, b_o, b_ig, b_fg, b_z, mlstm_conv[l], mlstm_ig_b[l],
                          mlstm_fg_b[l], mlstm_norm_g[l]).astype(dt)
        y_c = rwkv7_mixer(c_r, c_k, c_v, c_wd, c_ad, c_z, rwkv_mu[l], rwkv_w0[l], rwkv_w_up[l],
                          rwkv_a0[l], rwkv_a_up[l], rwkv_k_k[l], rwkv_k_a[l], rwkv_r_k[l],
                          rwkv_ln_g[l], rwkv_ln_b[l]).astype(dt)
        merged = (jax.nn.sigmoid(g_a) * (y_a @ w_br[l, 0])
                  + jax.nn.sigmoid(g_b) * (y_b @ w_br[l, 1])
                  + jax.nn.sigmoid(g_c) * (y_c @ w_br[l, 2]))
        h = h + merged @ w_out[l]
    return rmsnorm(h, final_norm_g)[:, N_META:]
```

```python
import functools

import numpy as np
import jax
import jax.numpy as jnp
from jax import lax
from jax.experimental import pallas as pl
from jax.experimental.pallas import tpu as pltpu

F32 = jnp.float32
BF16 = jnp.bfloat16

N_META = 16
CHUNK = 64
NORM_EPS = 1e-6
NEG_BIG = -1e30
F_FLOOR = 1e-12
BRANCH_WIDTH = 1024
A_HEADS, A_DK = 8, 128
B_HEADS, B_DQK, B_DV, B_CONV = 4, 128, 256, 4
C_HEADS, C_DH, C_RANK = 16, 64, 64
C_LN_EPS = 64e-5

LANES = 128
SUBLANES = 8
PAD_FRONT = 2 * CHUNK - N_META
VMEM_LIMIT = 56 * 1024 * 1024

A_Q, A_F, A_I, A_Z = 0, 1024, 2048, 3072
B_Q, B_K, B_V, B_O, B_Z = 4096, 4608, 5120, 6144, 7168
C_R, C_K, C_V, C_Z = 8192, 9216, 10240, 11264
G_A, G_B, G_C = 12288, 14336, 16384
B_G = 18432
C_LR = 18560
N_PROJ = 18944


def _divisor_tile(n, target, mult):
    best = None
    for d in range(mult, min(n, target) + 1, mult):
        if n % d == 0:
            best = d
    if best is None:
        raise ValueError(f"no tile for {n} (multiple of {mult}, <= {target})")
    return best


def _params(sem):
    return pltpu.CompilerParams(dimension_semantics=sem, vmem_limit_bytes=VMEM_LIMIT)


def _silu(x):
    return x * jax.nn.sigmoid(x)


def _softplus(x):
    return jnp.maximum(x, 0.0) + jnp.log(1.0 + jnp.exp(-jnp.abs(x)))


def _dot(a, b):
    return jnp.dot(a, b, preferred_element_type=F32)


def _dot_nt(a, b):
    return lax.dot_general(a, b, (((1,), (1,)), ((), ())), preferred_element_type=F32)


def _split3(x):
    hi = x.astype(BF16)
    r = x - hi.astype(F32)
    mid = r.astype(BF16)
    lo = (r - mid.astype(F32)).astype(BF16)
    return hi, mid, lo


def _cumsum_rows(tri, x):
    w = x.shape[1]
    hi, mid, lo = _split3(x)
    r = _dot(tri, jnp.concatenate([hi, mid, lo], axis=1))
    return (r[:, 2 * w:] + r[:, w:2 * w]) + r[:, :w]


def _row_bcast(x, r, n):
    return jnp.broadcast_to(x[r:r + 1, :], (n, x.shape[1]))


def _shift_rows(x, tail, j):
    rolled = pltpu.roll(x, j, 0)
    tail_r = pltpu.roll(tail, j, 0)
    row = lax.broadcasted_iota(jnp.int32, tail.shape, 0)
    first = jnp.where(row < j, tail_r, rolled[:SUBLANES])
    return jnp.concatenate([first, rolled[SUBLANES:]], axis=0)


def _rmsnorm_kernel(x_ref, g_ref, o_ref):
    x = x_ref[...]
    ms = jnp.mean(x * x, axis=-1, keepdims=True)
    o_ref[...] = (x * lax.rsqrt(ms + NORM_EPS) * g_ref[...]).astype(o_ref.dtype)


def _rmsnorm(h, g, out_dtype, tm):
    m, d = h.shape
    return pl.pallas_call(
        _rmsnorm_kernel,
        grid=(m // tm,),
        in_specs=[pl.BlockSpec((tm, d), lambda i: (i, 0)), pl.BlockSpec((1, d), lambda i: (0, 0))],
        out_specs=pl.BlockSpec((tm, d), lambda i: (i, 0)),
        out_shape=jax.ShapeDtypeStruct((m, d), out_dtype),
        compiler_params=_params(("arbitrary",)),
        name="rmsnorm",
    )(h, g)


def _final_norm_kernel(x_ref, g_ref, o_ref):
    x = x_ref[...]
    ms = jnp.mean(x * x, axis=-1, keepdims=True)
    o_ref[0] = x * lax.rsqrt(ms + NORM_EPS) * g_ref[...]


def _final_norm(h, g, bsz, t_pad, seq):
    d = h.shape[1]
    tm = CHUNK * 2
    per_b = t_pad // tm
    return pl.pallas_call(
        _final_norm_kernel,
        grid=(bsz, seq // tm),
        in_specs=[pl.BlockSpec((tm, d), lambda b, i: (b * per_b + i + 1, 0)),
                  pl.BlockSpec((1, d), lambda b, i: (0, 0))],
        out_specs=pl.BlockSpec((1, tm, d), lambda b, i: (b, i, 0)),
        out_shape=jax.ShapeDtypeStruct((bsz, seq, d), F32),
        compiler_params=_params(("arbitrary", "arbitrary")),
        name="final_norm",
    )(h, g)


def _inproj_kernel(x_ref, w_ref, o_ref):
    o_ref[...] = _dot(x_ref[...], w_ref[...])


def _inproj(xn, w, tm, tn):
    m, d = xn.shape
    n = w.shape[1]
    return pl.pallas_call(
        _inproj_kernel,
        grid=(m // tm, n // tn),
        in_specs=[pl.BlockSpec((tm, d), lambda i, j: (i, 0)), pl.BlockSpec((d, tn), lambda i, j: (0, j))],
        out_specs=pl.BlockSpec((tm, tn), lambda i, j: (i, j)),
        out_shape=jax.ShapeDtypeStruct((m, n), F32),
        compiler_params=_params(("arbitrary", "arbitrary")),
        name="inproj",
    )(xn, w)


def _hgrn2_levels():
    t = np.arange(CHUNK)[:, None]
    s = np.arange(CHUNK)[None, :]
    lv = np.zeros((CHUNK, CHUNK), np.int32)
    lv[(t // 4 == s // 4) & (s <= t)] = 1
    for lid, sz in ((2, 4), (3, 8), (4, 16), (5, 32)):
        lv[(t // (2 * sz) == s // (2 * sz)) & (t % (2 * sz) >= sz) & (s % (2 * sz) < sz)] = lid
    return lv


def _hgrn2_kernel(q_ref, f_ref, i_ref, z_ref, lb_ref, g_ref, tri_ref, lv_ref, o_ref, st_ref, *, heads, chunks):
    @pl.when(pl.program_id(2) == 0)
    def _():
        st_ref[...] = jnp.zeros_like(st_ref)

    def chunk(c, carry):
        r0 = pl.multiple_of(c * CHUNK, CHUNK)
        rows = pl.ds(r0, CHUNK)
        tri = tri_ref[...]
        lv = lv_ref[...]
        sub = lax.broadcasted_iota(jnp.int32, (SUBLANES, LANES), 0)
        for j in range(heads):
            ls = slice(j * LANES, (j + 1) * LANES)
            lb = lb_ref[:, ls]
            fr = f_ref[rows, ls]
            v = i_ref[rows, ls]
            qs = _silu(q_ref[rows, ls]) * (A_DK ** -0.5)
            k = (1.0 - lb) * jax.nn.sigmoid(-fr)
            lf = jnp.log(jnp.maximum(lb + (1.0 - lb) * jax.nn.sigmoid(fr), F_FLOOR))
            cg = _cumsum_rows(tri, lf)
            st = st_ref[j]
            o = _dot_nt((qs * jnp.exp(cg)).astype(BF16), st.astype(BF16))
            blocks = CHUNK // SUBLANES
            ref = jnp.concatenate(
                [jnp.where(sub < 4, _row_bcast(cg, 8 * g, SUBLANES), _row_bcast(cg, 8 * g + 4, SUBLANES))
                 for g in range(blocks)], axis=0)
            a = _dot_nt((qs * jnp.exp(cg - ref)).astype(BF16), (k * jnp.exp(ref - cg)).astype(BF16))
            att = jnp.where(lv == 1, a, 0.0)
            for lid, sz in ((2, 4), (3, 8), (4, 16), (5, 32)):
                ref = jnp.concatenate(
                    [_row_bcast(cg, g * 2 * sz + sz - 1, 2 * sz) for g in range(CHUNK // (2 * sz))], axis=0)
                qt = qs * jnp.exp(jnp.minimum(cg - ref, 0.0))
                kt = k * jnp.exp(jnp.minimum(ref - cg, 0.0))
                a = _dot_nt(qt.astype(BF16), kt.astype(BF16))
                att = jnp.where(lv == lid, a, att)
            o = o + _dot(att.astype(BF16), v.astype(BF16))
            cl = cg[CHUNK - 1:CHUNK, :]
            kd = k * jnp.exp(cl - cg)
            st_ref[j] = st * jnp.exp(cl) + _dot(v.T.astype(BF16), kd.astype(BF16))
            y = o * lax.rsqrt(jnp.mean(o * o, axis=-1, keepdims=True) + NORM_EPS) * g_ref[:, ls]
            o_ref[rows, ls] = (y * _silu(z_ref[rows, ls])).astype(o_ref.dtype)
        return carry

    lax.fori_loop(0, chunks, chunk, 0)


def _hgrn2(proj, lb, g, tri, lv, bsz, t_pad, tb, hw):
    m = proj.shape[0]
    nt = t_pad // tb
    ng = BRANCH_WIDTH // hw

    def col(off):
        return pl.BlockSpec((tb, hw), lambda b, gi, t: (b * nt + t, off // hw + gi))

    row = pl.BlockSpec((1, hw), lambda b, gi, t: (0, gi))
    const = pl.BlockSpec((CHUNK, CHUNK), lambda b, gi, t: (0, 0))
    kern = functools.partial(_hgrn2_kernel, heads=hw // LANES, chunks=tb // CHUNK)
    return pl.pallas_call(
        kern,
        grid=(bsz, ng, nt),
        in_specs=[col(A_Q), col(A_F), col(A_I), col(A_Z), row, row, const, const],
        out_specs=pl.BlockSpec((tb, hw), lambda b, gi, t: (b * nt + t, gi)),
        out_shape=jax.ShapeDtypeStruct((m, BRANCH_WIDTH), BF16),
        scratch_shapes=[pltpu.VMEM((hw // LANES, LANES, LANES), F32)],
        compiler_params=_params(("arbitrary", "arbitrary", "arbitrary")),
        name="hgrn2",
    )(proj, proj, proj, proj, lb, g, tri, lv)


def _mlstm_kernel(q_ref, k_ref, v_ref, og_ref, z_ref, gt_ref, cwq_ref, cwk_ref, gb_ref, ng_ref, tri_ref,
                  o_ref, c_ref, m_ref, tq_ref, tk_ref, *, heads, chunks):
    gi = pl.program_id(1)

    @pl.when(pl.program_id(2) == 0)
    def _():
        c_ref[...] = jnp.zeros_like(c_ref)
        m_ref[...] = jnp.zeros_like(m_ref)
        tq_ref[...] = jnp.zeros_like(tq_ref)
        tk_ref[...] = jnp.zeros_like(tk_ref)

    def conv(x, tail, w):
        acc = x * w[B_CONV - 1:B_CONV, :]
        for j in range(1, B_CONV):
            acc = acc + _shift_rows(x, tail, j) * w[B_CONV - 1 - j:B_CONV - j, :]
        return _silu(acc)

    def chunk(c, carry):
        r0 = pl.multiple_of(c * CHUNK, CHUNK)
        rows = pl.ds(r0, CHUNK)
        tri = tri_ref[...]
        t_i = lax.broadcasted_iota(jnp.int32, (CHUNK, CHUNK), 0)
        s_i = lax.broadcasted_iota(jnp.int32, (CHUNK, CHUNK), 1)
        causal = s_i <= t_i
        lane = lax.broadcasted_iota(jnp.int32, (CHUNK, LANES), 1)

        qraw = q_ref[rows, :]
        kraw = k_ref[rows, :]
        qc = conv(qraw, tq_ref[...], cwq_ref[...]) * (B_DQK ** -0.5)
        kc = conv(kraw, tk_ref[...], cwk_ref[...])
        tq_ref[...] = qraw[CHUNK - SUBLANES:, :]
        tk_ref[...] = kraw[CHUNK - SUBLANES:, :]

        gates = gt_ref[rows, :] + gb_ref[...]
        lf = jnp.minimum(gates, 0.0) - jnp.log(1.0 + jnp.exp(-jnp.abs(gates)))
        bcum = _cumsum_rows(tri, lf)
        gates_t = gates.T
        bcum_t = bcum.T
        for j in range(heads):
            ls = slice(j * B_DQK, (j + 1) * B_DQK)
            vs = slice(j * B_DV, (j + 1) * B_DV)
            ig_sel = lane == gi * heads + j
            fg_sel = lane == B_HEADS + gi * heads + j
            ig_col = jnp.sum(jnp.where(ig_sel, gates, 0.0), axis=1, keepdims=True)
            b_col = jnp.sum(jnp.where(fg_sel, bcum, 0.0), axis=1, keepdims=True)
            sub = lax.broadcasted_iota(jnp.int32, (LANES, CHUNK), 0)
            ig_row = jnp.sum(jnp.where(sub == gi * heads + j, gates_t, 0.0), axis=0, keepdims=True)
            b_row = jnp.sum(jnp.where(sub == B_HEADS + gi * heads + j, bcum_t, 0.0), axis=0, keepdims=True)

            q = qc[:, ls]
            k = kc[:, ls]
            v_aug = jnp.concatenate([v_ref[rows, vs], jnp.ones((CHUNK, LANES), F32)], axis=1).astype(BF16)
            m_prev = m_ref[j][0:1, 0:1]
            c_aug = c_ref[j]

            log_w = jnp.where(causal, b_col - b_row + ig_row, NEG_BIG)
            log_inter = b_col + m_prev
            m_t = jnp.maximum(log_inter, jnp.max(log_w, axis=-1, keepdims=True))
            scores = _dot_nt(q.astype(BF16), k.astype(BF16)) * jnp.exp(log_w - m_t)
            inter = jnp.exp(log_inter - m_t)
            numden = _dot(scores.astype(BF16), v_aug) + inter * _dot(q.astype(BF16), c_aug.astype(BF16))
            num = numden[:, :B_DV]
            den = numden[:, B_DV:B_DV + 1]
            hid = num / jnp.maximum(jnp.abs(den), jnp.exp(-m_t))

            b_end = b_col[CHUNK - 1:CHUNK, :]
            log_s = b_end - b_col + ig_col
            m_new = jnp.maximum(b_end + m_prev, jnp.max(log_s, axis=0, keepdims=True))
            w_s = jnp.exp(log_s - m_new)
            carry_c = jnp.exp(b_end + m_prev - m_new)
            c_ref[j] = carry_c * c_aug + _dot((k * w_s).T.astype(BF16), v_aug)
            m_ref[j] = jnp.broadcast_to(m_new, (SUBLANES, LANES))

            hc = hid - jnp.mean(hid, axis=-1, keepdims=True)
            hn = hc * lax.rsqrt(jnp.mean(hc * hc, axis=-1, keepdims=True) + NORM_EPS) * ng_ref[:, vs]
            y = hn * jax.nn.sigmoid(og_ref[rows, vs]) * _silu(z_ref[rows, vs])
            o_ref[rows, vs] = y.astype(o_ref.dtype)
        return carry

    lax.fori_loop(0, chunks, chunk, 0)


def _mlstm(proj, conv_w, gate_b, norm_g, tri, bsz, t_pad, tb, hb):
    m = proj.shape[0]
    nt = t_pad // tb
    ng = B_HEADS // hb
    qw = hb * B_DQK
    vw = hb * B_DV

    def col(off, w):
        return pl.BlockSpec((tb, w), lambda b, gi, t: (b * nt + t, off // w + gi))

    kern = functools.partial(_mlstm_kernel, heads=hb, chunks=tb // CHUNK)
    return pl.pallas_call(
        kern,
        grid=(bsz, ng, nt),
        in_specs=[col(B_Q, qw), col(B_K, qw), col(B_V, vw), col(B_O, vw), col(B_Z, vw),
                  pl.BlockSpec((tb, LANES), lambda b, gi, t: (b * nt + t, B_G // LANES)),
                  pl.BlockSpec((B_CONV, qw), lambda b, gi, t: (0, gi)),
                  pl.BlockSpec((B_CONV, qw), lambda b, gi, t: (0, ng + gi)),
                  pl.BlockSpec((1, LANES), lambda b, gi, t: (0, 0)),
                  pl.BlockSpec((1, vw), lambda b, gi, t: (0, gi)),
                  pl.BlockSpec((CHUNK, CHUNK), lambda b, gi, t: (0, 0))],
        out_specs=pl.BlockSpec((tb, vw), lambda b, gi, t: (b * nt + t, gi)),
        out_shape=jax.ShapeDtypeStruct((m, BRANCH_WIDTH), BF16),
        scratch_shapes=[pltpu.VMEM((hb, B_DQK, B_DV + LANES), F32),
                        pltpu.VMEM((hb, SUBLANES, LANES), F32),
                        pltpu.VMEM((SUBLANES, qw), F32),
                        pltpu.VMEM((SUBLANES, qw), F32)],
        compiler_params=_params(("arbitrary", "arbitrary", "arbitrary")),
        name="mlstm",
    )(proj, proj, proj, proj, proj, proj, conv_w, conv_w, gate_b, norm_g, tri)


def _rwkv_kernel(r_ref, k_ref, v_ref, z_ref, lr_ref, mur_ref, muk_ref, muv_ref, mulr_ref, w0_ref, a0_ref,
                 kk_ref, ka_ref, rk_ref, lng_ref, lnb_ref, w2h_ref, w2l_ref, tri_ref, ones_ref,
                 o_ref, st_ref, tr_ref, tk_ref, tv_ref, tlr_ref, *, pairs, chunks):
    @pl.when(pl.program_id(2) == 0)
    def _():
        st_ref[...] = jnp.zeros_like(st_ref)
        tr_ref[...] = jnp.zeros_like(tr_ref)
        tk_ref[...] = jnp.zeros_like(tk_ref)
        tv_ref[...] = jnp.zeros_like(tv_ref)
        tlr_ref[...] = jnp.zeros_like(tlr_ref)

    def lerp(x, tail, mu):
        return x + (_shift_rows(x, tail, 1) - x) * mu

    def chunk(c, carry):
        r0 = pl.multiple_of(c * CHUNK, CHUNK)
        rows = pl.ds(r0, CHUNK)
        tri = tri_ref[...]
        ones_bd = ones_ref[...]
        lane = lax.broadcasted_iota(jnp.int32, (CHUNK, LANES), 1)
        t_i = lax.broadcasted_iota(jnp.int32, (CHUNK, LANES), 0)
        s_i = lane % C_DH
        head0 = lane < C_DH
        strict = s_i < t_i
        incl = s_i <= t_i
        bd_r = lax.broadcasted_iota(jnp.int32, (LANES, LANES), 0) // C_DH
        bd_c = lax.broadcasted_iota(jnp.int32, (LANES, LANES), 1) // C_DH
        blockdiag = bd_r == bd_c

        def stack(x):
            return jnp.concatenate([jnp.where(head0, x, 0.0), jnp.where(head0, 0.0, x)], axis=0)

        def gsum(x):
            hi = x.astype(BF16)
            lo = (x - hi.astype(F32)).astype(BF16)
            r = _dot(jnp.concatenate([hi, lo], axis=0), ones_bd)
            return r[:CHUNK] + r[CHUNK:]

        lr_raw = lr_ref[rows, :]
        lr = lerp(lr_raw, tlr_ref[...], mulr_ref[...])
        tlr_ref[...] = lr_raw[CHUNK - SUBLANES:, :]
        lr = jnp.where(head0, jnp.tanh(lr), lr)
        lr_hi = lr.astype(BF16)
        lr_lo = (lr - lr_hi.astype(F32)).astype(BF16)
        lr_st = jnp.concatenate([lr_hi, lr_lo], axis=0)

        for j in range(pairs):
            ls = slice(j * LANES, (j + 1) * LANES)
            us = slice(j * 2 * LANES, (j + 1) * 2 * LANES)
            r_raw = r_ref[rows, ls]
            k_raw = k_ref[rows, ls]
            v_raw = v_ref[rows, ls]
            r = lerp(r_raw, tr_ref[:, ls], mur_ref[:, ls])
            k0 = lerp(k_raw, tk_ref[:, ls], muk_ref[:, ls])
            v = lerp(v_raw, tv_ref[:, ls], muv_ref[:, ls])
            tr_ref[:, ls] = r_raw[CHUNK - SUBLANES:, :]
            tk_ref[:, ls] = k_raw[CHUNK - SUBLANES:, :]
            tv_ref[:, ls] = v_raw[CHUNK - SUBLANES:, :]

            up2 = _dot(lr_st, w2h_ref[:, us])
            up = up2[:CHUNK] + up2[CHUNK:] + _dot(lr_hi, w2l_ref[:, us])
            w_log = -_softplus(-(w0_ref[:, ls] + up[:, :LANES])) - 0.5
            lw = -jnp.exp(w_log)
            a = jax.nn.sigmoid(a0_ref[:, ls] + up[:, LANES:])
            kk = k0 * kk_ref[:, ls]
            kk = kk / jnp.maximum(jnp.sqrt(gsum(kk * kk)), 1e-12)
            k = k0 * (1.0 + (a - 1.0) * ka_ref[:, ls])
            b = kk * a

            cw = _cumsum_rows(tri, lw)
            e_c = jnp.exp(cw)
            e_n = jnp.exp(-cw)
            at = -kk * jnp.exp(cw - lw)
            rt = r * e_c
            bt = b * e_n
            kt = k * e_n
            ar = jnp.concatenate([at, rt], axis=0).astype(BF16)
            bk = jnp.concatenate([stack(bt), stack(kt)], axis=0).astype(BF16)
            p = _dot_nt(ar, bk)
            m_ab = jnp.where(strict, p[:CHUNK, :LANES], 0.0)
            l_ak = jnp.where(strict, p[:CHUNK, LANES:], 0.0)
            r_b = jnp.where(incl, p[CHUNK:, :LANES], 0.0)
            r_k = jnp.where(incl, p[CHUNK:, LANES:], 0.0)

            st = st_ref[j]
            a_s = _dot_nt(ar, st.astype(BF16))
            v_st = stack(v).astype(BF16)
            u = a_s[:CHUNK] + _dot(l_ak.astype(BF16), v_st)
            mk = m_ab
            steps = CHUNK.bit_length() - 1
            for it in range(steps):
                if it + 1 < steps:
                    x = jnp.concatenate([stack(u), stack(mk)], axis=1).astype(BF16)
                    res = _dot(mk.astype(BF16), x)
                    u = u + res[:, :LANES]
                    mk = res[:, LANES:]
                else:
                    u = u + _dot(mk.astype(BF16), stack(u).astype(BF16))
            u_st = stack(u).astype(BF16)
            y = a_s[CHUNK:] + _dot(jnp.concatenate([r_b, r_k], axis=1).astype(BF16),
                                   jnp.concatenate([u_st, v_st], axis=0))
            uv = jnp.concatenate([u, v], axis=0)
            bkt = jnp.concatenate([bt, kt], axis=0)
            delta = _dot(uv.T.astype(BF16), bkt.astype(BF16))
            st_ref[j] = (st + jnp.where(blockdiag, delta, 0.0)) * e_c[CHUNK - 1:CHUNK, :]

            mean = gsum(y) * (1.0 / C_DH)
            yc = y - mean
            var = gsum(yc * yc) * (1.0 / C_DH)
            yn = yc * lax.rsqrt(var + C_LN_EPS) * lng_ref[:, ls] + lnb_ref[:, ls]
            bonus = gsum(r * k * rk_ref[:, ls]) * v
            o_ref[rows, ls] = ((yn + bonus) * _silu(z_ref[rows, ls])).astype(o_ref.dtype)
        return carry

    lax.fori_loop(0, chunks, chunk, 0)


def _rwkv(proj, rows_cw, mu_lr, w2h, w2l, tri, ones_bd, bsz, t_pad, tb, hp):
    m = proj.shape[0]
    nt = t_pad // tb
    cw = hp * LANES
    ng = BRANCH_WIDTH // cw

    def col(off):
        return pl.BlockSpec((tb, cw), lambda b, gi, t: (b * nt + t, off // cw + gi))

    row = pl.BlockSpec((1, cw), lambda b, gi, t: (0, gi))
    kern = functools.partial(_rwkv_kernel, pairs=hp, chunks=tb // CHUNK)
    return pl.pallas_call(
        kern,
        grid=(bsz, ng, nt),
        in_specs=[col(C_R), col(C_K), col(C_V), col(C_Z),
                  pl.BlockSpec((tb, LANES), lambda b, gi, t: (b * nt + t, C_LR // LANES)),
                  row, row, row, pl.BlockSpec((1, LANES), lambda b, gi, t: (0, 0)),
                  row, row, row, row, row, row, row,
                  pl.BlockSpec((LANES, 2 * cw), lambda b, gi, t: (0, gi)),
                  pl.BlockSpec((LANES, 2 * cw), lambda b, gi, t: (0, gi)),
                  pl.BlockSpec((CHUNK, CHUNK), lambda b, gi, t: (0, 0)),
                  pl.BlockSpec((LANES, LANES), lambda b, gi, t: (0, 0))],
        out_specs=pl.BlockSpec((tb, cw), lambda b, gi, t: (b * nt + t, gi)),
        out_shape=jax.ShapeDtypeStruct((m, BRANCH_WIDTH), BF16),
        scratch_shapes=[pltpu.VMEM((hp, LANES, LANES), F32),
                        pltpu.VMEM((SUBLANES, cw), F32),
                        pltpu.VMEM((SUBLANES, cw), F32),
                        pltpu.VMEM((SUBLANES, cw), F32),
                        pltpu.VMEM((SUBLANES, LANES), F32)],
        compiler_params=_params(("arbitrary", "arbitrary", "arbitrary")),
        name="rwkv7",
    )(proj, proj, proj, proj, proj, *rows_cw[:3], mu_lr, *rows_cw[3:], w2h, w2l, tri, ones_bd)


def _merge_kernel(ya_ref, yb_ref, yc_ref, ga_ref, gb_ref, gc_ref, w_ref, o_ref):
    acc = jax.nn.sigmoid(ga_ref[...]) * _dot(ya_ref[...], w_ref[0])
    acc = acc + jax.nn.sigmoid(gb_ref[...]) * _dot(yb_ref[...], w_ref[1])
    acc = acc + jax.nn.sigmoid(gc_ref[...]) * _dot(yc_ref[...], w_ref[2])
    o_ref[...] = acc.astype(o_ref.dtype)


def _merge(ya, yb, yc, proj, w_br, tm, tn):
    m = ya.shape[0]
    d = w_br.shape[2]
    y_spec = pl.BlockSpec((tm, BRANCH_WIDTH), lambda i, j: (i, 0))

    def gate(off):
        return pl.BlockSpec((tm, tn), lambda i, j: (i, off // tn + j))

    return pl.pallas_call(
        _merge_kernel,
        grid=(m // tm, d // tn),
        in_specs=[y_spec, y_spec, y_spec, gate(G_A), gate(G_B), gate(G_C),
                  pl.BlockSpec((3, BRANCH_WIDTH, tn), lambda i, j: (0, 0, j))],
        out_specs=pl.BlockSpec((tm, tn), lambda i, j: (i, j)),
        out_shape=jax.ShapeDtypeStruct((m, d), BF16),
        compiler_params=_params(("arbitrary", "arbitrary")),
        name="merge",
    )(ya, yb, yc, proj, proj, proj, w_br)


def _outproj_kernel(h_ref, x_ref, w_ref, o_ref):
    new = h_ref[...] + _dot(x_ref[...], w_ref[...])
    row = lax.broadcasted_iota(jnp.int32, new.shape, 0)
    keep = jnp.logical_or(pl.program_id(1) > 0, row >= PAD_FRONT)
    o_ref[...] = jnp.where(keep, new, 0.0)


def _outproj(h, merged, w_out, bsz, t_pad, tm, tn):
    m, d = h.shape
    nt = t_pad // tm
    return pl.pallas_call(
        _outproj_kernel,
        grid=(bsz, nt, d // tn),
        in_specs=[pl.BlockSpec((tm, tn), lambda b, t, j: (b * nt + t, j)),
                  pl.BlockSpec((tm, d), lambda b, t, j: (b * nt + t, 0)),
                  pl.BlockSpec((d, tn), lambda b, t, j: (0, j))],
        out_specs=pl.BlockSpec((tm, tn), lambda b, t, j: (b * nt + t, j)),
        out_shape=jax.ShapeDtypeStruct((m, d), F32),
        input_output_aliases={0: 0},
        compiler_params=_params(("arbitrary", "arbitrary", "arbitrary")),
        name="outproj",
    )(h, merged, w_out)


def _regroup_w_in(w_in):
    depth, d, _ = w_in.shape
    src = {}
    off = 0
    for name, width in (("a_q", 1024), ("a_f", 1024), ("a_i", 1024), ("a_z", 1024),
                        ("b_q", 512), ("b_k", 512), ("b_v", 1024), ("b_o", 1024), ("b_g", 8), ("b_z", 1024),
                        ("c_r", 1024), ("c_k", 1024), ("c_v", 1024), ("c_lr", 128), ("c_z", 1024),
                        ("g_a", 2048), ("g_b", 2048), ("g_c", 2048)):
        src[name] = (off, width)
        off += width
    assert off == w_in.shape[2]

    def take(name):
        o, w = src[name]
        return w_in[:, :, o:o + w]

    zeros = lambda w: jnp.zeros((depth, d, w), w_in.dtype)
    parts = [take(n) for n in ("a_q", "a_f", "a_i", "a_z", "b_q", "b_k", "b_v", "b_o", "b_z",
                               "c_r", "c_k", "c_v", "c_z", "g_a", "g_b", "g_c")]
    parts += [take("b_g"), zeros(LANES - 2 * B_HEADS), take("c_lr"), zeros(N_PROJ - C_LR - LANES)]
    return jnp.concatenate(parts, axis=-1).astype(BF16)


def _rwkv_up_weights(w_up, a_up):
    depth = w_up.shape[0]
    npairs = BRANCH_WIDTH // LANES
    wu = w_up.reshape(depth, C_RANK, npairs, LANES)
    au = a_up.reshape(depth, C_RANK, npairs, LANES)
    top = jnp.concatenate([wu, jnp.zeros_like(wu)], axis=-1)
    bot = jnp.concatenate([jnp.zeros_like(au), au], axis=-1)
    w2 = jnp.concatenate([top, bot], axis=1).reshape(depth, 2 * C_RANK, npairs * 2 * LANES)
    hi = w2.astype(BF16)
    lo = (w2 - hi.astype(F32)).astype(BF16)
    return hi, lo


def kernel(x, meta_tokens, norm_g, w_in, hgrn_lb_logits, hgrn_norm_g, mlstm_conv, mlstm_ig_b, mlstm_fg_b,
           mlstm_norm_g, rwkv_mu, rwkv_w0, rwkv_w_up, rwkv_a0, rwkv_a_up, rwkv_k_k, rwkv_k_a, rwkv_r_k,
           rwkv_ln_g, rwkv_ln_b, w_br, w_out, final_norm_g):
    bsz, seq, d = x.shape
    depth = w_in.shape[0]
    t_pad = PAD_FRONT + N_META + seq
    m = bsz * t_pad
    assert seq % (2 * CHUNK) == 0 and t_pad % CHUNK == 0

    tb = _divisor_tile(t_pad, 640, CHUNK)
    tm_in = _divisor_tile(m, 1664, LANES)
    tm_row = _divisor_tile(m, 640, LANES)
    tm_out = _divisor_tile(t_pad, 1280, LANES)

    w_in_p = _regroup_w_in(w_in)
    w_br_b = w_br.astype(BF16)
    w_out_b = w_out.astype(BF16)
    p = jax.nn.softmax(hgrn_lb_logits.astype(F32), axis=0)
    lower_bounds = jnp.cumsum(p, axis=0) - p[0]
    gate_b = jnp.concatenate([mlstm_ig_b, mlstm_fg_b, jnp.zeros((depth, LANES - 2 * B_HEADS), F32)], axis=-1)
    w2h, w2l = _rwkv_up_weights(rwkv_w_up, rwkv_a_up)
    w = BRANCH_WIDTH
    mu_r, mu_k, mu_v, mu_lr = rwkv_mu[:, :w], rwkv_mu[:, w:2 * w], rwkv_mu[:, 2 * w:3 * w], rwkv_mu[:, 3 * w:]
    tri = jnp.asarray(np.tril(np.ones((CHUNK, CHUNK), np.float32)), BF16)
    levels = jnp.asarray(_hgrn2_levels())
    lane_head = np.arange(LANES) // C_DH
    ones_bd = jnp.asarray((lane_head[:, None] == lane_head[None, :]).astype(np.float32), BF16)

    meta = jnp.broadcast_to(meta_tokens[None].astype(F32), (bsz, N_META, d))
    h = jnp.concatenate([jnp.zeros((bsz, PAD_FRONT, d), F32), meta, x.astype(F32)], axis=1).reshape(m, d)

    for l in range(depth):
        row = lambda a: a[l][None, :]
        xn = _rmsnorm(h, row(norm_g), BF16, tm_row)
        proj = _inproj(xn, w_in_p[l], tm_in, 512)
        ya = _hgrn2(proj, row(lower_bounds), row(hgrn_norm_g), tri, levels, bsz, t_pad, tb, 2 * LANES)
        yb = _mlstm(proj, mlstm_conv[l], row(gate_b), row(mlstm_norm_g), tri, bsz, t_pad, tb, 2)
        rows_cw = [row(a) for a in (mu_r, mu_k, mu_v, rwkv_w0, rwkv_a0, rwkv_k_k, rwkv_k_a, rwkv_r_k,
                                    rwkv_ln_g, rwkv_ln_b)]
        yc = _rwkv(proj, rows_cw, row(mu_lr), w2h[l], w2l[l], tri, ones_bd, bsz, t_pad, tb, 4)
        merged = _merge(ya, yb, yc, proj, w_br_b[l], tm_row, 512)
        h = _outproj(h, merged, w_out_b[l], bsz, t_pad, tm_out, 512)

    return _final_norm(h, final_norm_g[None, :], bsz, t_pad, seq)
```

```python
import functools

import numpy as np
import jax
import jax.numpy as jnp
from jax import lax
from jax.experimental import pallas as pl
from jax.experimental.pallas import tpu as pltpu

F32 = jnp.float32
BF16 = jnp.bfloat16

N_META = 16
CHUNK = 64
NORM_EPS = 1e-6
NEG_BIG = -1e30
F_FLOOR = 1e-12
BRANCH_WIDTH = 1024
A_HEADS, A_DK = 8, 128
B_HEADS, B_DQK, B_DV, B_CONV = 4, 128, 256, 4
C_HEADS, C_DH, C_RANK = 16, 64, 64
C_LN_EPS = 64e-5

LANES = 128
SUBLANES = 8
PAD_FRONT = 2 * CHUNK - N_META
VMEM_LIMIT = 56 * 1024 * 1024

A_Q, A_F, A_I, A_Z = 0, 1024, 2048, 3072
B_Q, B_K, B_V, B_O, B_Z = 4096, 4608, 5120, 6144, 7168
C_R, C_K, C_V, C_Z = 8192, 9216, 10240, 11264
G_A, G_B, G_C = 12288, 14336, 16384
B_G = 18432
C_LR = 18560
N_PROJ = 18944


def _divisor_tile(n, target, mult):
    best = None
    for d in range(mult, min(n, target) + 1, mult):
        if n % d == 0:
            best = d
    if best is None:
        raise ValueError(f"no tile for {n} (multiple of {mult}, <= {target})")
    return best


def _params(sem):
    return pltpu.CompilerParams(dimension_semantics=sem, vmem_limit_bytes=VMEM_LIMIT)


def _silu(x):
    return x * jax.nn.sigmoid(x)


def _softplus(x):
    return jnp.maximum(x, 0.0) + jnp.log(1.0 + jnp.exp(-jnp.abs(x)))


def _dot(a, b):
    return jnp.dot(a, b, preferred_element_type=F32)


def _dot_nt(a, b):
    return lax.dot_general(a, b, (((1,), (1,)), ((), ())), preferred_element_type=F32)


def _split3(x):
    hi = x.astype(BF16)
    r = x - hi.astype(F32)
    mid = r.astype(BF16)
    lo = (r - mid.astype(F32)).astype(BF16)
    return hi, mid, lo


def _cumsum_rows(tri, x):
    w = x.shape[1]
    hi, mid, lo = _split3(x)
    r = _dot(tri, jnp.concatenate([hi, mid, lo], axis=1))
    return (r[:, 2 * w:] + r[:, w:2 * w]) + r[:, :w]


def _row_bcast(x, r, n):
    return jnp.broadcast_to(x[r:r + 1, :], (n, x.shape[1]))


def _shift_rows(x, tail, j):
    rolled = pltpu.roll(x, j, 0)
    tail_r = pltpu.roll(tail, j, 0)
    row = lax.broadcasted_iota(jnp.int32, tail.shape, 0)
    first = jnp.where(row < j, tail_r, rolled[:SUBLANES])
    return jnp.concatenate([first, rolled[SUBLANES:]], axis=0)


def _tiles(x, w=LANES):
    return [x[:, i * w:(i + 1) * w] for i in range(x.shape[1] // w)]


def _rmsnorm_kernel(x_ref, g_ref, o_ref):
    x = x_ref[...]
    ms = jnp.mean(x * x, axis=-1, keepdims=True)
    o_ref[...] = (x * lax.rsqrt(ms + NORM_EPS) * g_ref[...]).astype(o_ref.dtype)


def _rmsnorm(h, g, out_dtype, tm):
    m, d = h.shape
    return pl.pallas_call(
        _rmsnorm_kernel,
        grid=(m // tm,),
        in_specs=[pl.BlockSpec((tm, d), lambda i: (i, 0)), pl.BlockSpec((1, d), lambda i: (0, 0))],
        out_specs=pl.BlockSpec((tm, d), lambda i: (i, 0)),
        out_shape=jax.ShapeDtypeStruct((m, d), out_dtype),
        compiler_params=_params(("arbitrary",)),
        name="rmsnorm",
    )(h, g)


def _final_norm_kernel(x_ref, g_ref, o_ref):
    x = x_ref[...]
    ms = jnp.mean(x * x, axis=-1, keepdims=True)
    o_ref[0] = x * lax.rsqrt(ms + NORM_EPS) * g_ref[...]


def _final_norm(h, g, bsz, t_pad, seq):
    d = h.shape[1]
    tm = CHUNK * 2
    per_b = t_pad // tm
    return pl.pallas_call(
        _final_norm_kernel,
        grid=(bsz, seq // tm),
        in_specs=[pl.BlockSpec((tm, d), lambda b, i: (b * per_b + i + 1, 0)),
                  pl.BlockSpec((1, d), lambda b, i: (0, 0))],
        out_specs=pl.BlockSpec((1, tm, d), lambda b, i: (b, i, 0)),
        out_shape=jax.ShapeDtypeStruct((bsz, seq, d), F32),
        compiler_params=_params(("arbitrary", "arbitrary")),
        name="final_norm",
    )(h, g)


def _inproj_kernel(x_ref, w_ref, o_ref):
    o_ref[...] = _dot(x_ref[...], w_ref[...])


def _inproj(xn, w, tm, tn):
    m, d = xn.shape
    n = w.shape[1]
    return pl.pallas_call(
        _inproj_kernel,
        grid=(m // tm, n // tn),
        in_specs=[pl.BlockSpec((tm, d), lambda i, j: (i, 0)), pl.BlockSpec((d, tn), lambda i, j: (0, j))],
        out_specs=pl.BlockSpec((tm, tn), lambda i, j: (i, j)),
        out_shape=jax.ShapeDtypeStruct((m, n), F32),
        compiler_params=_params(("arbitrary", "arbitrary")),
        name="inproj",
    )(xn, w)


def _hgrn2_levels():
    t = np.arange(CHUNK)[:, None]
    s = np.arange(CHUNK)[None, :]
    lv = np.zeros((CHUNK, CHUNK), np.int32)
    lv[(t // 4 == s // 4) & (s <= t)] = 1
    for lid, sz in ((2, 4), (3, 8), (4, 16), (5, 32)):
        lv[(t // (2 * sz) == s // (2 * sz)) & (t % (2 * sz) >= sz) & (s % (2 * sz) < sz)] = lid
    return lv


def _hgrn2_kernel(q_ref, f_ref, i_ref, z_ref, lb_ref, g_ref, tri_ref, lv_ref, o_ref, st_ref, *, chunks):
    heads = range(A_HEADS)

    @pl.when(pl.program_id(1) == 0)
    def _():
        st_ref[...] = jnp.zeros_like(st_ref)

    def chunk(c, carry):
        rows = pl.ds(pl.multiple_of(c * CHUNK, CHUNK), CHUNK)
        lv = lv_ref[...]
        sub = lax.broadcasted_iota(jnp.int32, (SUBLANES, BRANCH_WIDTH), 0)
        lb = lb_ref[...]
        fr = f_ref[rows, :]
        v = i_ref[rows, :]
        qs = _silu(q_ref[rows, :]) * (A_DK ** -0.5)
        k = (1.0 - lb) * jax.nn.sigmoid(-fr)
        lf = jnp.log(jnp.maximum(lb + (1.0 - lb) * jax.nn.sigmoid(fr), F_FLOOR))
        cg = _cumsum_rows(tri_ref[...], lf)
        cl = cg[CHUNK - 1:CHUNK, :]
        st = [st_ref[j] for j in heads]
        qg = _tiles((qs * jnp.exp(cg)).astype(BF16))
        o = [_dot_nt(qg[j], st[j].astype(BF16)) for j in heads]

        ref = jnp.concatenate(
            [jnp.where(sub < 4, _row_bcast(cg, 8 * g, SUBLANES), _row_bcast(cg, 8 * g + 4, SUBLANES))
             for g in range(CHUNK // SUBLANES)], axis=0)
        qt = _tiles((qs * jnp.exp(cg - ref)).astype(BF16))
        kt = _tiles((k * jnp.exp(ref - cg)).astype(BF16))
        att = [jnp.where(lv == 1, _dot_nt(qt[j], kt[j]), 0.0) for j in heads]
        for lid, sz in ((2, 4), (3, 8), (4, 16), (5, 32)):
            ref = jnp.concatenate(
                [_row_bcast(cg, g * 2 * sz + sz - 1, 2 * sz) for g in range(CHUNK // (2 * sz))], axis=0)
            qt = _tiles((qs * jnp.exp(jnp.minimum(cg - ref, 0.0))).astype(BF16))
            kt = _tiles((k * jnp.exp(jnp.minimum(ref - cg, 0.0))).astype(BF16))
            att = [jnp.where(lv == lid, _dot_nt(qt[j], kt[j]), att[j]) for j in heads]

        vb = _tiles(v.astype(BF16))
        o = [o[j] + _dot(att[j].astype(BF16), vb[j]) for j in heads]
        kd = _tiles((k * jnp.exp(cl - cg)).astype(BF16))
        vt = [t.T.astype(BF16) for t in _tiles(v)]
        decay = _tiles(jnp.exp(cl))
        for j in heads:
            st_ref[j] = st[j] * decay[j] + _dot(vt[j], kd[j])
        on = [o[j] * lax.rsqrt(jnp.mean(o[j] * o[j], axis=-1, keepdims=True) + NORM_EPS) for j in heads]
        y = jnp.concatenate(on, axis=1) * g_ref[...] * _silu(z_ref[rows, :])
        o_ref[rows, :] = y.astype(o_ref.dtype)
        return carry

    lax.fori_loop(0, chunks, chunk, 0)


def _hgrn2(proj, lb, g, tri, lv, bsz, t_pad, tb):
    m = proj.shape[0]
    nt = t_pad // tb
    w = BRANCH_WIDTH

    def col(off):
        return pl.BlockSpec((tb, w), lambda b, t: (b * nt + t, off // w))

    row = pl.BlockSpec((1, w), lambda b, t: (0, 0))
    const = pl.BlockSpec((CHUNK, CHUNK), lambda b, t: (0, 0))
    return pl.pallas_call(
        functools.partial(_hgrn2_kernel, chunks=tb // CHUNK),
        grid=(bsz, nt),
        in_specs=[col(A_Q), col(A_F), col(A_I), col(A_Z), row, row, const, const],
        out_specs=pl.BlockSpec((tb, w), lambda b, t: (b * nt + t, 0)),
        out_shape=jax.ShapeDtypeStruct((m, w), BF16),
        scratch_shapes=[pltpu.VMEM((A_HEADS, LANES, LANES), F32)],
        compiler_params=_params(("arbitrary", "arbitrary")),
        name="hgrn2",
    )(proj, proj, proj, proj, lb, g, tri, lv)


def _mlstm_kernel(q_ref, k_ref, v_ref, og_ref, z_ref, gt_ref, cwq_ref, cwk_ref, gb_ref, ng_ref, tri_ref,
                  o_ref, c_ref, m_ref, tq_ref, tk_ref, *, chunks):
    heads = range(B_HEADS)

    @pl.when(pl.program_id(1) == 0)
    def _():
        c_ref[...] = jnp.zeros_like(c_ref)
        m_ref[...] = jnp.zeros_like(m_ref)
        tq_ref[...] = jnp.zeros_like(tq_ref)
        tk_ref[...] = jnp.zeros_like(tk_ref)

    def conv(x, tail, w):
        acc = x * w[B_CONV - 1:B_CONV, :]
        for j in range(1, B_CONV):
            acc = acc + _shift_rows(x, tail, j) * w[B_CONV - 1 - j:B_CONV - j, :]
        return _silu(acc)

    def chunk(c, carry):
        rows = pl.ds(pl.multiple_of(c * CHUNK, CHUNK), CHUNK)
        t_i = lax.broadcasted_iota(jnp.int32, (CHUNK, CHUNK), 0)
        s_i = lax.broadcasted_iota(jnp.int32, (CHUNK, CHUNK), 1)
        causal = s_i <= t_i
        lane = lax.broadcasted_iota(jnp.int32, (CHUNK, LANES), 1)
        sub = lax.broadcasted_iota(jnp.int32, (LANES, CHUNK), 0)

        qraw = q_ref[rows, :]
        kraw = k_ref[rows, :]
        qc = _tiles((conv(qraw, tq_ref[...], cwq_ref[...]) * (B_DQK ** -0.5)).astype(BF16), B_DQK)
        kc32 = _tiles(conv(kraw, tk_ref[...], cwk_ref[...]), B_DQK)
        kc = [t.astype(BF16) for t in kc32]
        tq_ref[...] = qraw[CHUNK - SUBLANES:, :]
        tk_ref[...] = kraw[CHUNK - SUBLANES:, :]

        gates = gt_ref[rows, :] + gb_ref[...]
        lf = jnp.minimum(gates, 0.0) - jnp.log(1.0 + jnp.exp(-jnp.abs(gates)))
        bcum = _cumsum_rows(tri_ref[...], lf)
        gates_t = gates.T
        bcum_t = bcum.T
        ig_col = [jnp.sum(jnp.where(lane == j, gates, 0.0), axis=1, keepdims=True) for j in heads]
        b_col = [jnp.sum(jnp.where(lane == B_HEADS + j, bcum, 0.0), axis=1, keepdims=True) for j in heads]
        ig_row = [jnp.sum(jnp.where(sub == j, gates_t, 0.0), axis=0, keepdims=True) for j in heads]
        b_row = [jnp.sum(jnp.where(sub == B_HEADS + j, bcum_t, 0.0), axis=0, keepdims=True) for j in heads]

        ones = jnp.ones((CHUNK, LANES), BF16)
        v_aug = [jnp.concatenate([t.astype(BF16), ones], axis=1) for t in _tiles(v_ref[rows, :], B_DV)]
        m_prev = [m_ref[j][0:1, 0:1] for j in heads]
        c_aug = [c_ref[j] for j in heads]

        qk = [_dot_nt(qc[j], kc[j]) for j in heads]
        qcs = [_dot(qc[j], c_aug[j].astype(BF16)) for j in heads]
        log_w = [jnp.where(causal, b_col[j] - b_row[j] + ig_row[j], NEG_BIG) for j in heads]
        log_inter = [b_col[j] + m_prev[j] for j in heads]
        m_t = [jnp.maximum(log_inter[j], jnp.max(log_w[j], axis=-1, keepdims=True)) for j in heads]
        scores = [(qk[j] * jnp.exp(log_w[j] - m_t[j])).astype(BF16) for j in heads]
        numden = [_dot(scores[j], v_aug[j]) + jnp.exp(log_inter[j] - m_t[j]) * qcs[j] for j in heads]
        hid = [numden[j][:, :B_DV] / jnp.maximum(jnp.abs(numden[j][:, B_DV:B_DV + 1]), jnp.exp(-m_t[j]))
               for j in heads]

        b_end = [b_col[j][CHUNK - 1:CHUNK, :] for j in heads]
        log_s = [b_end[j] - b_col[j] + ig_col[j] for j in heads]
        m_new = [jnp.maximum(b_end[j] + m_prev[j], jnp.max(log_s[j], axis=0, keepdims=True)) for j in heads]
        kw = [(kc32[j] * jnp.exp(log_s[j] - m_new[j])).T.astype(BF16) for j in heads]
        for j in heads:
            c_ref[j] = jnp.exp(b_end[j] + m_prev[j] - m_new[j]) * c_aug[j] + _dot(kw[j], v_aug[j])
            m_ref[j] = jnp.broadcast_to(m_new[j], (SUBLANES, LANES))

        hc = [hid[j] - jnp.mean(hid[j], axis=-1, keepdims=True) for j in heads]
        hn = [hc[j] * lax.rsqrt(jnp.mean(hc[j] * hc[j], axis=-1, keepdims=True) + NORM_EPS) for j in heads]
        y = jnp.concatenate(hn, axis=1) * ng_ref[...] * jax.nn.sigmoid(og_ref[rows, :]) * _silu(z_ref[rows, :])
        o_ref[rows, :] = y.astype(o_ref.dtype)
        return carry

    lax.fori_loop(0, chunks, chunk, 0)


def _mlstm(proj, conv_w, gate_b, norm_g, tri, bsz, t_pad, tb):
    m = proj.shape[0]
    nt = t_pad // tb
    qw = B_HEADS * B_DQK
    vw = B_HEADS * B_DV

    def col(off, w):
        return pl.BlockSpec((tb, w), lambda b, t: (b * nt + t, off // w))

    return pl.pallas_call(
        functools.partial(_mlstm_kernel, chunks=tb // CHUNK),
        grid=(bsz, nt),
        in_specs=[col(B_Q, qw), col(B_K, qw), col(B_V, vw), col(B_O, vw), col(B_Z, vw), col(B_G, LANES),
                  pl.BlockSpec((B_CONV, qw), lambda b, t: (0, 0)),
                  pl.BlockSpec((B_CONV, qw), lambda b, t: (0, 1)),
                  pl.BlockSpec((1, LANES), lambda b, t: (0, 0)),
                  pl.BlockSpec((1, vw), lambda b, t: (0, 0)),
                  pl.BlockSpec((CHUNK, CHUNK), lambda b, t: (0, 0))],
        out_specs=pl.BlockSpec((tb, vw), lambda b, t: (b * nt + t, 0)),
        out_shape=jax.ShapeDtypeStruct((m, BRANCH_WIDTH), BF16),
        scratch_shapes=[pltpu.VMEM((B_HEADS, B_DQK, B_DV + LANES), F32),
                        pltpu.VMEM((B_HEADS, SUBLANES, LANES), F32),
                        pltpu.VMEM((SUBLANES, qw), F32),
                        pltpu.VMEM((SUBLANES, qw), F32)],
        compiler_params=_params(("arbitrary", "arbitrary")),
        name="mlstm",
    )(proj, proj, proj, proj, proj, proj, conv_w, conv_w, gate_b, norm_g, tri)


def _rwkv_kernel(r_ref, k_ref, v_ref, z_ref, lr_ref, mur_ref, muk_ref, muv_ref, mulr_ref, w0_ref, a0_ref,
                 kk_ref, ka_ref, rk_ref, lng_ref, lnb_ref, w2h_ref, w2l_ref, tri_ref, ones_ref,
                 o_ref, st_ref, tr_ref, tk_ref, tv_ref, tlr_ref, *, chunks):
    npairs = BRANCH_WIDTH // LANES
    pairs = range(npairs)

    @pl.when(pl.program_id(1) == 0)
    def _():
        st_ref[...] = jnp.zeros_like(st_ref)
        tr_ref[...] = jnp.zeros_like(tr_ref)
        tk_ref[...] = jnp.zeros_like(tk_ref)
        tv_ref[...] = jnp.zeros_like(tv_ref)
        tlr_ref[...] = jnp.zeros_like(tlr_ref)

    def lerp(x_ref, tail_ref, mu_ref, rows):
        x = x_ref[rows, :]
        out = x + (_shift_rows(x, tail_ref[...], 1) - x) * mu_ref[...]
        tail_ref[...] = x[CHUNK - SUBLANES:, :]
        return out

    def chunk(c, carry):
        rows = pl.ds(pl.multiple_of(c * CHUNK, CHUNK), CHUNK)
        tri = tri_ref[...]
        ones_bd = ones_ref[...]
        lane = lax.broadcasted_iota(jnp.int32, (CHUNK, LANES), 1)
        t_i = lax.broadcasted_iota(jnp.int32, (CHUNK, LANES), 0)
        s_i = lane % C_DH
        head0 = lane < C_DH
        strict = s_i < t_i
        incl = s_i <= t_i
        bd_r = lax.broadcasted_iota(jnp.int32, (LANES, LANES), 0) // C_DH
        bd_c = lax.broadcasted_iota(jnp.int32, (LANES, LANES), 1) // C_DH
        blockdiag = bd_r == bd_c

        def stack(x):
            return jnp.concatenate([jnp.where(head0, x, 0.0), jnp.where(head0, 0.0, x)], axis=0)

        def gsum(x):
            hi = x.astype(BF16)
            lo = (x - hi.astype(F32)).astype(BF16)
            res = _dot(jnp.concatenate(_tiles(hi) + _tiles(lo), axis=0), ones_bd)
            half = npairs * CHUNK
            res = res[:half] + res[half:]
            return jnp.concatenate([res[p * CHUNK:(p + 1) * CHUNK] for p in pairs], axis=1)

        lr = lerp(lr_ref, tlr_ref, mulr_ref, rows)
        lr = jnp.where(head0, jnp.tanh(lr), lr)
        lr_hi = lr.astype(BF16)
        lr_lo = (lr - lr_hi.astype(F32)).astype(BF16)
        up2 = _dot(jnp.concatenate([lr_hi, lr_lo], axis=0), w2h_ref[...])
        up = up2[:CHUNK] + up2[CHUNK:] + _dot(lr_hi, w2l_ref[...])

        r = lerp(r_ref, tr_ref, mur_ref, rows)
        k0 = lerp(k_ref, tk_ref, muk_ref, rows)
        v = lerp(v_ref, tv_ref, muv_ref, rows)
        w_log = -_softplus(-(w0_ref[...] + up[:, :BRANCH_WIDTH])) - 0.5
        lw = -jnp.exp(w_log)
        a = jax.nn.sigmoid(a0_ref[...] + up[:, BRANCH_WIDTH:])
        kk = k0 * kk_ref[...]
        kk = kk / jnp.maximum(jnp.sqrt(gsum(kk * kk)), 1e-12)
        k = k0 * (1.0 + (a - 1.0) * ka_ref[...])
        b = kk * a

        cw = _cumsum_rows(tri, lw)
        e_c = jnp.exp(cw)
        e_n = jnp.exp(-cw)
        at = _tiles(-kk * jnp.exp(cw - lw))
        rt = _tiles(r * e_c)
        bt = _tiles(b * e_n)
        kt = _tiles(k * e_n)
        vp = _tiles(v)
        decay = _tiles(e_c[CHUNK - 1:CHUNK, :])
        st = [st_ref[p] for p in pairs]

        ar = [jnp.concatenate([at[p], rt[p]], axis=0).astype(BF16) for p in pairs]
        bk = [jnp.concatenate([stack(bt[p]), stack(kt[p])], axis=0).astype(BF16) for p in pairs]
        pm = [_dot_nt(ar[p], bk[p]) for p in pairs]
        a_s = [_dot_nt(ar[p], st[p].astype(BF16)) for p in pairs]
        v_st = [stack(vp[p]).astype(BF16) for p in pairs]
        l_ak = [jnp.where(strict, pm[p][:CHUNK, LANES:], 0.0).astype(BF16) for p in pairs]
        u = [a_s[p][:CHUNK] + _dot(l_ak[p], v_st[p]) for p in pairs]
        mk = [jnp.where(strict, pm[p][:CHUNK, :LANES], 0.0) for p in pairs]
        steps = CHUNK.bit_length() - 1
        for it in range(steps - 1):
            res = [_dot(mk[p].astype(BF16),
                        jnp.concatenate([stack(u[p]), stack(mk[p])], axis=1).astype(BF16)) for p in pairs]
            u = [u[p] + res[p][:, :LANES] for p in pairs]
            mk = [res[p][:, LANES:] for p in pairs]
        u = [u[p] + _dot(mk[p].astype(BF16), stack(u[p]).astype(BF16)) for p in pairs]

        rbk = [jnp.concatenate([jnp.where(incl, pm[p][CHUNK:, :LANES], 0.0),
                                jnp.where(incl, pm[p][CHUNK:, LANES:], 0.0)], axis=1).astype(BF16) for p in pairs]
        y = [a_s[p][CHUNK:] + _dot(rbk[p], jnp.concatenate([stack(u[p]).astype(BF16), v_st[p]], axis=0))
             for p in pairs]
        uvt = [jnp.concatenate([u[p], vp[p]], axis=0).T.astype(BF16) for p in pairs]
        bkt = [jnp.concatenate([bt[p], kt[p]], axis=0).astype(BF16) for p in pairs]
        for p in pairs:
            st_ref[p] = (st[p] + jnp.where(blockdiag, _dot(uvt[p], bkt[p]), 0.0)) * decay[p]

        y = jnp.concatenate(y, axis=1)
        yc = y - gsum(y) * (1.0 / C_DH)
        var = gsum(yc * yc) * (1.0 / C_DH)
        yn = yc * lax.rsqrt(var + C_LN_EPS) * lng_ref[...] + lnb_ref[...]
        bonus = gsum(r * k * rk_ref[...]) * v
        o_ref[rows, :] = ((yn + bonus) * _silu(z_ref[rows, :])).astype(o_ref.dtype)
        return carry

    lax.fori_loop(0, chunks, chunk, 0)


def _rwkv(proj, rows_cw, mu_lr, w2h, w2l, tri, ones_bd, bsz, t_pad, tb):
    m = proj.shape[0]
    nt = t_pad // tb
    w = BRANCH_WIDTH

    def col(off, width):
        return pl.BlockSpec((tb, width), lambda b, t: (b * nt + t, off // width))

    def whole(shape):
        return pl.BlockSpec(shape, lambda b, t: (0, 0))

    row = whole((1, w))
    return pl.pallas_call(
        functools.partial(_rwkv_kernel, chunks=tb // CHUNK),
        grid=(bsz, nt),
        in_specs=[col(C_R, w), col(C_K, w), col(C_V, w), col(C_Z, w), col(C_LR, LANES),
                  row, row, row, whole((1, LANES)), row, row, row, row, row, row, row,
                  whole((LANES, 2 * w)), whole((LANES, 2 * w)), whole((CHUNK, CHUNK)), whole((LANES, LANES))],
        out_specs=pl.BlockSpec((tb, w), lambda b, t: (b * nt + t, 0)),
        out_shape=jax.ShapeDtypeStruct((m, w), BF16),
        scratch_shapes=[pltpu.VMEM((w // LANES, LANES, LANES), F32),
                        pltpu.VMEM((SUBLANES, w), F32),
                        pltpu.VMEM((SUBLANES, w), F32),
                        pltpu.VMEM((SUBLANES, w), F32),
                        pltpu.VMEM((SUBLANES, LANES), F32)],
        compiler_params=_params(("arbitrary", "arbitrary")),
        name="rwkv7",
    )(proj, proj, proj, proj, proj, *rows_cw[:3], mu_lr, *rows_cw[3:], w2h, w2l, tri, ones_bd)


def _merge_kernel(ya_ref, yb_ref, yc_ref, ga_ref, gb_ref, gc_ref, w_ref, o_ref):
    acc = jax.nn.sigmoid(ga_ref[...]) * _dot(ya_ref[...], w_ref[0])
    acc = acc + jax.nn.sigmoid(gb_ref[...]) * _dot(yb_ref[...], w_ref[1])
    acc = acc + jax.nn.sigmoid(gc_ref[...]) * _dot(yc_ref[...], w_ref[2])
    o_ref[...] = acc.astype(o_ref.dtype)


def _merge(ya, yb, yc, proj, w_br, tm, tn):
    m = ya.shape[0]
    d = w_br.shape[2]
    y_spec = pl.BlockSpec((tm, BRANCH_WIDTH), lambda i, j: (i, 0))

    def gate(off):
        return pl.BlockSpec((tm, tn), lambda i, j: (i, off // tn + j))

    return pl.pallas_call(
        _merge_kernel,
        grid=(m // tm, d // tn),
        in_specs=[y_spec, y_spec, y_spec, gate(G_A), gate(G_B), gate(G_C),
                  pl.BlockSpec((3, BRANCH_WIDTH, tn), lambda i, j: (0, 0, j))],
        out_specs=pl.BlockSpec((tm, tn), lambda i, j: (i, j)),
        out_shape=jax.ShapeDtypeStruct((m, d), BF16),
        compiler_params=_params(("arbitrary", "arbitrary")),
        name="merge",
    )(ya, yb, yc, proj, proj, proj, w_br)


def _outproj_kernel(h_ref, x_ref, w_ref, o_ref):
    new = h_ref[...] + _dot(x_ref[...], w_ref[...])
    row = lax.broadcasted_iota(jnp.int32, new.shape, 0)
    keep = jnp.logical_or(pl.program_id(1) > 0, row >= PAD_FRONT)
    o_ref[...] = jnp.where(keep, new, 0.0)


def _outproj(h, merged, w_out, bsz, t_pad, tm, tn):
    m, d = h.shape
    nt = t_pad // tm
    return pl.pallas_call(
        _outproj_kernel,
        grid=(bsz, nt, d // tn),
        in_specs=[pl.BlockSpec((tm, tn), lambda b, t, j: (b * nt + t, j)),
                  pl.BlockSpec((tm, d), lambda b, t, j: (b * nt + t, 0)),
                  pl.BlockSpec((d, tn), lambda b, t, j: (0, j))],
        out_specs=pl.BlockSpec((tm, tn), lambda b, t, j: (b * nt + t, j)),
        out_shape=jax.ShapeDtypeStruct((m, d), F32),
        input_output_aliases={0: 0},
        compiler_params=_params(("arbitrary", "arbitrary", "arbitrary")),
        name="outproj",
    )(h, merged, w_out)


def _regroup_w_in(w_in):
    depth, d, _ = w_in.shape
    src = {}
    off = 0
    for name, width in (("a_q", 1024), ("a_f", 1024), ("a_i", 1024), ("a_z", 1024),
                        ("b_q", 512), ("b_k", 512), ("b_v", 1024), ("b_o", 1024), ("b_g", 8), ("b_z", 1024),
                        ("c_r", 1024), ("c_k", 1024), ("c_v", 1024), ("c_lr", 128), ("c_z", 1024),
                        ("g_a", 2048), ("g_b", 2048), ("g_c", 2048)):
        src[name] = (off, width)
        off += width
    assert off == w_in.shape[2]

    def take(name):
        o, w = src[name]
        return w_in[:, :, o:o + w]

    zeros = lambda w: jnp.zeros((depth, d, w), w_in.dtype)
    parts = [take(n) for n in ("a_q", "a_f", "a_i", "a_z", "b_q", "b_k", "b_v", "b_o", "b_z",
                               "c_r", "c_k", "c_v", "c_z", "g_a", "g_b", "g_c")]
    parts += [take("b_g"), zeros(LANES - 2 * B_HEADS), take("c_lr"), zeros(N_PROJ - C_LR - LANES)]
    return jnp.concatenate(parts, axis=-1).astype(BF16)


def _rwkv_up_weights(w_up, a_up):
    top = jnp.concatenate([w_up, jnp.zeros_like(w_up)], axis=-1)
    bot = jnp.concatenate([jnp.zeros_like(a_up), a_up], axis=-1)
    w2 = jnp.concatenate([top, bot], axis=1)
    hi = w2.astype(BF16)
    lo = (w2 - hi.astype(F32)).astype(BF16)
    return hi, lo


def kernel(x, meta_tokens, norm_g, w_in, hgrn_lb_logits, hgrn_norm_g, mlstm_conv, mlstm_ig_b, mlstm_fg_b,
           mlstm_norm_g, rwkv_mu, rwkv_w0, rwkv_w_up, rwkv_a0, rwkv_a_up, rwkv_k_k, rwkv_k_a, rwkv_r_k,
           rwkv_ln_g, rwkv_ln_b, w_br, w_out, final_norm_g):
    bsz, seq, d = x.shape
    depth = w_in.shape[0]
    t_pad = PAD_FRONT + N_META + seq
    m = bsz * t_pad
    assert seq % (2 * CHUNK) == 0 and t_pad % CHUNK == 0

    tb = _divisor_tile(t_pad, 640, CHUNK)
    tm_in = _divisor_tile(m, 1664, LANES)
    tm_row = _divisor_tile(m, 640, LANES)
    tm_out = _divisor_tile(t_pad, 1280, LANES)

    w_in_p = _regroup_w_in(w_in)
    w_br_b = w_br.astype(BF16)
    w_out_b = w_out.astype(BF16)
    p = jax.nn.softmax(hgrn_lb_logits.astype(F32), axis=0)
    lower_bounds = jnp.cumsum(p, axis=0) - p[0]
    gate_b = jnp.concatenate([mlstm_ig_b, mlstm_fg_b, jnp.zeros((depth, LANES - 2 * B_HEADS), F32)], axis=-1)
    w2h, w2l = _rwkv_up_weights(rwkv_w_up, rwkv_a_up)
    w = BRANCH_WIDTH
    mu_r, mu_k, mu_v, mu_lr = rwkv_mu[:, :w], rwkv_mu[:, w:2 * w], rwkv_mu[:, 2 * w:3 * w], rwkv_mu[:, 3 * w:]
    tri = jnp.asarray(np.tril(np.ones((CHUNK, CHUNK), np.float32)), BF16)
    levels = jnp.asarray(_hgrn2_levels())
    lane_head = np.arange(LANES) // C_DH
    ones_bd = jnp.asarray((lane_head[:, None] == lane_head[None, :]).astype(np.float32), BF16)

    meta = jnp.broadcast_to(meta_tokens[None].astype(F32), (bsz, N_META, d))
    h = jnp.concatenate([jnp.zeros((bsz, PAD_FRONT, d), F32), meta, x.astype(F32)], axis=1).reshape(m, d)

    for l in range(depth):
        row = lambda a: a[l][None, :]
        xn = _rmsnorm(h, row(norm_g), BF16, tm_row)
        proj = _inproj(xn, w_in_p[l], tm_in, 512)
        ya = _hgrn2(proj, row(lower_bounds), row(hgrn_norm_g), tri, levels, bsz, t_pad, tb)
        yb = _mlstm(proj, mlstm_conv[l], row(gate_b), row(mlstm_norm_g), tri, bsz, t_pad, tb)
        rows_cw = [row(a) for a in (mu_r, mu_k, mu_v, rwkv_w0, rwkv_a0, rwkv_k_k, rwkv_k_a, rwkv_r_k,
                                    rwkv_ln_g, rwkv_ln_b)]
        yc = _rwkv(proj, rows_cw, row(mu_lr), w2h[l], w2l[l], tri, ones_bd, bsz, t_pad, tb)
        merged = _merge(ya, yb, yc, proj, w_br_b[l], tm_row, 512)
        h = _outproj(h, merged, w_out_b[l], bsz, t_pad, tm_out, 512)

    return _final_norm(h, final_norm_g[None, :], bsz, t_pad, seq)
```

```python
import functools

import numpy as np
import jax
import jax.numpy as jnp
from jax import lax
from jax.experimental import pallas as pl
from jax.experimental.pallas import tpu as pltpu

F32 = jnp.float32
BF16 = jnp.bfloat16

N_META = 16
CHUNK = 64
NORM_EPS = 1e-6
NEG_BIG = -1e30
F_FLOOR = 1e-12
BRANCH_WIDTH = 1024
A_HEADS, A_DK = 8, 128
B_HEADS, B_DQK, B_DV, B_CONV = 4, 128, 256, 4
C_HEADS, C_DH, C_RANK = 16, 64, 64
C_LN_EPS = 64e-5

LANES = 128
SUBLANES = 8
PAD_FRONT = 2 * CHUNK - N_META
VMEM_LIMIT = 56 * 1024 * 1024

A_Q, A_F, A_I, A_Z = 0, 1024, 2048, 3072
B_Q, B_K, B_V, B_O, B_Z = 4096, 4608, 5120, 6144, 7168
C_R, C_K, C_V, C_Z = 8192, 9216, 10240, 11264
G_A, G_B, G_C = 12288, 14336, 16384
B_G = 18432
C_LR = 18560
N_PROJ = 18944


def _divisor_tile(n, target, mult):
    best = None
    for d in range(mult, min(n, target) + 1, mult):
        if n % d == 0:
            best = d
    if best is None:
        raise ValueError(f"no tile for {n} (multiple of {mult}, <= {target})")
    return best


def _params(sem):
    return pltpu.CompilerParams(dimension_semantics=sem, vmem_limit_bytes=VMEM_LIMIT)


def _sigmoid(x):
    return 0.5 * jnp.tanh(0.5 * x) + 0.5


def _silu(x):
    return x * _sigmoid(x)


def _softplus(x):
    return jnp.maximum(x, 0.0) + jnp.log(1.0 + jnp.exp(-jnp.abs(x)))


def _dot(a, b):
    return jnp.dot(a, b, preferred_element_type=F32)


def _dot_nt(a, b):
    return lax.dot_general(a, b, (((1,), (1,)), ((), ())), preferred_element_type=F32)


def _split3(x):
    hi = x.astype(BF16)
    r = x - hi.astype(F32)
    mid = r.astype(BF16)
    lo = (r - mid.astype(F32)).astype(BF16)
    return hi, mid, lo


def _cumsum_rows(tri, x):
    w = x.shape[1]
    hi, mid, lo = _split3(x)
    r = _dot(tri, jnp.concatenate([hi, mid, lo], axis=1))
    return (r[:, 2 * w:] + r[:, w:2 * w]) + r[:, :w]


def _row_bcast(x, r, n):
    return jnp.broadcast_to(x[r:r + 1, :], (n, x.shape[1]))


def _shift_rows(x, tail, j):
    rolled = pltpu.roll(x, j, 0)
    tail_r = pltpu.roll(tail, j, 0)
    row = lax.broadcasted_iota(jnp.int32, tail.shape, 0)
    first = jnp.where(row < j, tail_r, rolled[:SUBLANES])
    return jnp.concatenate([first, rolled[SUBLANES:]], axis=0)


def _tiles(x, w=LANES):
    return [x[:, i * w:(i + 1) * w] for i in range(x.shape[1] // w)]


def _rmsnorm_kernel(x_ref, g_ref, o_ref):
    x = x_ref[...]
    ms = jnp.mean(x * x, axis=-1, keepdims=True)
    o_ref[...] = (x * lax.rsqrt(ms + NORM_EPS) * g_ref[...]).astype(o_ref.dtype)


def _rmsnorm(h, g, out_dtype, tm):
    m, d = h.shape
    return pl.pallas_call(
        _rmsnorm_kernel,
        grid=(m // tm,),
        in_specs=[pl.BlockSpec((tm, d), lambda i: (i, 0)), pl.BlockSpec((1, d), lambda i: (0, 0))],
        out_specs=pl.BlockSpec((tm, d), lambda i: (i, 0)),
        out_shape=jax.ShapeDtypeStruct((m, d), out_dtype),
        compiler_params=_params(("arbitrary",)),
        name="rmsnorm",
    )(h, g)


def _final_norm_kernel(x_ref, g_ref, o_ref):
    x = x_ref[...]
    ms = jnp.mean(x * x, axis=-1, keepdims=True)
    o_ref[0] = x * lax.rsqrt(ms + NORM_EPS) * g_ref[...]


def _final_norm(h, g, bsz, t_pad, seq):
    d = h.shape[1]
    tm = CHUNK * 2
    per_b = t_pad // tm
    return pl.pallas_call(
        _final_norm_kernel,
        grid=(bsz, seq // tm),
        in_specs=[pl.BlockSpec((tm, d), lambda b, i: (b * per_b + i + 1, 0)),
                  pl.BlockSpec((1, d), lambda b, i: (0, 0))],
        out_specs=pl.BlockSpec((1, tm, d), lambda b, i: (b, i, 0)),
        out_shape=jax.ShapeDtypeStruct((bsz, seq, d), F32),
        compiler_params=_params(("arbitrary", "arbitrary")),
        name="final_norm",
    )(h, g)


def _inproj_kernel(x_ref, w_ref, o_ref):
    o_ref[...] = _dot(x_ref[...], w_ref[...])


def _inproj(xn, w, layer, tm, tn):
    m, d = xn.shape
    n = w.shape[2]
    return pl.pallas_call(
        _inproj_kernel,
        grid=(m // tm, n // tn),
        in_specs=[pl.BlockSpec((tm, d), lambda i, j: (i, 0)),
                  pl.BlockSpec((None, d, tn), lambda i, j: (layer, 0, j))],
        out_specs=pl.BlockSpec((tm, tn), lambda i, j: (i, j)),
        out_shape=jax.ShapeDtypeStruct((m, n), F32),
        compiler_params=_params(("arbitrary", "arbitrary")),
        name="inproj",
    )(xn, w)


def _hgrn2_levels():
    t = np.arange(CHUNK)[:, None]
    s = np.arange(CHUNK)[None, :]
    lv = np.zeros((CHUNK, CHUNK), np.int32)
    lv[(t // 4 == s // 4) & (s <= t)] = 1
    for lid, sz in ((2, 4), (3, 8), (4, 16), (5, 32)):
        lv[(t // (2 * sz) == s // (2 * sz)) & (t % (2 * sz) >= sz) & (s % (2 * sz) < sz)] = lid
    return lv


def _hgrn2_kernel(q_ref, f_ref, i_ref, z_ref, lb_ref, g_ref, tri_ref, lv_ref, o_ref, st_ref, *, chunks):
    heads = range(A_HEADS)

    @pl.when(pl.program_id(1) == 0)
    def _():
        st_ref[...] = jnp.zeros_like(st_ref)

    def chunk(c, carry):
        rows = pl.ds(pl.multiple_of(c * CHUNK, CHUNK), CHUNK)
        lv = lv_ref[...]
        sub = lax.broadcasted_iota(jnp.int32, (SUBLANES, BRANCH_WIDTH), 0)
        lb = lb_ref[...]
        fr = f_ref[rows, :]
        v = i_ref[rows, :]
        qs = _silu(q_ref[rows, :]) * (A_DK ** -0.5)
        sg = _sigmoid(fr)
        k = (1.0 - lb) * (1.0 - sg)
        lf = jnp.log(jnp.maximum(lb + (1.0 - lb) * sg, F_FLOOR))
        cg = _cumsum_rows(tri_ref[...], lf)
        cl = cg[CHUNK - 1:CHUNK, :]
        st = [st_ref[j] for j in heads]
        qg = _tiles((qs * jnp.exp(cg)).astype(BF16))
        o = [_dot_nt(qg[j], st[j].astype(BF16)) for j in heads]

        ref = jnp.concatenate(
            [jnp.where(sub < 4, _row_bcast(cg, 8 * g, SUBLANES), _row_bcast(cg, 8 * g + 4, SUBLANES))
             for g in range(CHUNK // SUBLANES)], axis=0)
        qt = _tiles((qs * jnp.exp(cg - ref)).astype(BF16))
        kt = _tiles((k * jnp.exp(ref - cg)).astype(BF16))
        att = [jnp.where(lv == 1, _dot_nt(qt[j], kt[j]), 0.0) for j in heads]
        row_i = lax.broadcasted_iota(jnp.int32, (CHUNK, BRANCH_WIDTH), 0)
        for lid, sz in ((2, 4), (3, 8), (4, 16), (5, 32)):
            ref = jnp.concatenate(
                [_row_bcast(cg, g * 2 * sz + sz - 1, 2 * sz) for g in range(CHUNK // (2 * sz))], axis=0)
            later = (row_i & sz) != 0
            x = _tiles((jnp.where(later, qs, k) * jnp.exp(-jnp.abs(cg - ref))).astype(BF16))
            att = [jnp.where(lv == lid, _dot_nt(x[j], x[j]), att[j]) for j in heads]

        vb = _tiles(v.astype(BF16))
        o = [o[j] + _dot(att[j].astype(BF16), vb[j]) for j in heads]
        kd = _tiles((k * jnp.exp(cl - cg)).astype(BF16))
        vt = [t.T.astype(BF16) for t in _tiles(v)]
        decay = _tiles(jnp.exp(cl))
        for j in heads:
            st_ref[j] = st[j] * decay[j] + _dot(vt[j], kd[j])
        on = [o[j] * lax.rsqrt(jnp.mean(o[j] * o[j], axis=-1, keepdims=True) + NORM_EPS) for j in heads]
        y = jnp.concatenate(on, axis=1) * g_ref[...] * _silu(z_ref[rows, :])
        o_ref[rows, :] = y.astype(o_ref.dtype)
        return carry

    lax.fori_loop(0, chunks, chunk, 0)


def _hgrn2(proj, lb, g, tri, lv, bsz, t_pad, tb):
    m = proj.shape[0]
    nt = t_pad // tb
    w = BRANCH_WIDTH

    def col(off):
        return pl.BlockSpec((tb, w), lambda b, t: (b * nt + t, off // w))

    row = pl.BlockSpec((1, w), lambda b, t: (0, 0))
    const = pl.BlockSpec((CHUNK, CHUNK), lambda b, t: (0, 0))
    return pl.pallas_call(
        functools.partial(_hgrn2_kernel, chunks=tb // CHUNK),
        grid=(bsz, nt),
        in_specs=[col(A_Q), col(A_F), col(A_I), col(A_Z), row, row, const, const],
        out_specs=pl.BlockSpec((tb, w), lambda b, t: (b * nt + t, 0)),
        out_shape=jax.ShapeDtypeStruct((m, w), BF16),
        scratch_shapes=[pltpu.VMEM((A_HEADS, LANES, LANES), F32)],
        compiler_params=_params(("arbitrary", "arbitrary")),
        name="hgrn2",
    )(proj, proj, proj, proj, lb, g, tri, lv)


def _mlstm_kernel(q_ref, k_ref, v_ref, og_ref, z_ref, gt_ref, cwq_ref, cwk_ref, gb_ref, ng_ref, tri_ref,
                  o_ref, c_ref, m_ref, tq_ref, tk_ref, *, chunks):
    heads = range(B_HEADS)

    @pl.when(pl.program_id(1) == 0)
    def _():
        c_ref[...] = jnp.zeros_like(c_ref)
        m_ref[...] = jnp.zeros_like(m_ref)
        tq_ref[...] = jnp.zeros_like(tq_ref)
        tk_ref[...] = jnp.zeros_like(tk_ref)

    def conv(x, tail, w):
        acc = x * w[B_CONV - 1:B_CONV, :]
        for j in range(1, B_CONV):
            acc = acc + _shift_rows(x, tail, j) * w[B_CONV - 1 - j:B_CONV - j, :]
        return _silu(acc)

    def chunk(c, carry):
        rows = pl.ds(pl.multiple_of(c * CHUNK, CHUNK), CHUNK)
        t_i = lax.broadcasted_iota(jnp.int32, (CHUNK, CHUNK), 0)
        s_i = lax.broadcasted_iota(jnp.int32, (CHUNK, CHUNK), 1)
        causal = s_i <= t_i
        lane = lax.broadcasted_iota(jnp.int32, (CHUNK, LANES), 1)
        sub = lax.broadcasted_iota(jnp.int32, (LANES, CHUNK), 0)

        qraw = q_ref[rows, :]
        kraw = k_ref[rows, :]
        qc = _tiles((conv(qraw, tq_ref[...], cwq_ref[...]) * (B_DQK ** -0.5)).astype(BF16), B_DQK)
        kc32 = _tiles(conv(kraw, tk_ref[...], cwk_ref[...]), B_DQK)
        kc = [t.astype(BF16) for t in kc32]
        tq_ref[...] = qraw[CHUNK - SUBLANES:, :]
        tk_ref[...] = kraw[CHUNK - SUBLANES:, :]

        gates = gt_ref[rows, :] + gb_ref[...]
        lf = jnp.minimum(gates, 0.0) - jnp.log(1.0 + jnp.exp(-jnp.abs(gates)))
        bcum = _cumsum_rows(tri_ref[...], lf)
        gates_t = gates.T
        bcum_t = bcum.T
        ig_col = [jnp.sum(jnp.where(lane == j, gates, 0.0), axis=1, keepdims=True) for j in heads]
        b_col = [jnp.sum(jnp.where(lane == B_HEADS + j, bcum, 0.0), axis=1, keepdims=True) for j in heads]
        ig_row = [jnp.sum(jnp.where(sub == j, gates_t, 0.0), axis=0, keepdims=True) for j in heads]
        b_row = [jnp.sum(jnp.where(sub == B_HEADS + j, bcum_t, 0.0), axis=0, keepdims=True) for j in heads]

        ones = jnp.ones((CHUNK, LANES), BF16)
        v_aug = [jnp.concatenate([t.astype(BF16), ones], axis=1) for t in _tiles(v_ref[rows, :], B_DV)]
        m_prev = [m_ref[j][0:1, 0:1] for j in heads]
        c_aug = [c_ref[j] for j in heads]

        qk = [_dot_nt(qc[j], kc[j]) for j in heads]
        qcs = [_dot(qc[j], c_aug[j].astype(BF16)) for j in heads]
        log_w = [jnp.where(causal, b_col[j] - b_row[j] + ig_row[j], NEG_BIG) for j in heads]
        log_inter = [b_col[j] + m_prev[j] for j in heads]
        m_t = [jnp.maximum(log_inter[j], jnp.max(log_w[j], axis=-1, keepdims=True)) for j in heads]
        scores = [(qk[j] * jnp.exp(log_w[j] - m_t[j])).astype(BF16) for j in heads]
        numden = [_dot(scores[j], v_aug[j]) + jnp.exp(log_inter[j] - m_t[j]) * qcs[j] for j in heads]
        hid = [numden[j][:, :B_DV] / jnp.maximum(jnp.abs(numden[j][:, B_DV:B_DV + 1]), jnp.exp(-m_t[j]))
               for j in heads]

        b_end = [b_col[j][CHUNK - 1:CHUNK, :] for j in heads]
        log_s = [b_end[j] - b_col[j] + ig_col[j] for j in heads]
        m_new = [jnp.maximum(b_end[j] + m_prev[j], jnp.max(log_s[j], axis=0, keepdims=True)) for j in heads]
        kw = [(kc32[j] * jnp.exp(log_s[j] - m_new[j])).T.astype(BF16) for j in heads]
        for j in heads:
            c_ref[j] = jnp.exp(b_end[j] + m_prev[j] - m_new[j]) * c_aug[j] + _dot(kw[j], v_aug[j])
            m_ref[j] = jnp.broadcast_to(m_new[j], (SUBLANES, LANES))

        hc = [hid[j] - jnp.mean(hid[j], axis=-1, keepdims=True) for j in heads]
        hn = [hc[j] * lax.rsqrt(jnp.mean(hc[j] * hc[j], axis=-1, keepdims=True) + NORM_EPS) for j in heads]
        y = jnp.concatenate(hn, axis=1) * ng_ref[...] * _sigmoid(og_ref[rows, :]) * _silu(z_ref[rows, :])
        o_ref[rows, :] = y.astype(o_ref.dtype)
        return carry

    lax.fori_loop(0, chunks, chunk, 0)


def _mlstm(proj, conv_w, gate_b, norm_g, tri, bsz, t_pad, tb):
    m = proj.shape[0]
    nt = t_pad // tb
    qw = B_HEADS * B_DQK
    vw = B_HEADS * B_DV

    def col(off, w):
        return pl.BlockSpec((tb, w), lambda b, t: (b * nt + t, off // w))

    return pl.pallas_call(
        functools.partial(_mlstm_kernel, chunks=tb // CHUNK),
        grid=(bsz, nt),
        in_specs=[col(B_Q, qw), col(B_K, qw), col(B_V, vw), col(B_O, vw), col(B_Z, vw), col(B_G, LANES),
                  pl.BlockSpec((B_CONV, qw), lambda b, t: (0, 0)),
                  pl.BlockSpec((B_CONV, qw), lambda b, t: (0, 1)),
                  pl.BlockSpec((1, LANES), lambda b, t: (0, 0)),
                  pl.BlockSpec((1, vw), lambda b, t: (0, 0)),
                  pl.BlockSpec((CHUNK, CHUNK), lambda b, t: (0, 0))],
        out_specs=pl.BlockSpec((tb, vw), lambda b, t: (b * nt + t, 0)),
        out_shape=jax.ShapeDtypeStruct((m, BRANCH_WIDTH), BF16),
        scratch_shapes=[pltpu.VMEM((B_HEADS, B_DQK, B_DV + LANES), F32),
                        pltpu.VMEM((B_HEADS, SUBLANES, LANES), F32),
                        pltpu.VMEM((SUBLANES, qw), F32),
                        pltpu.VMEM((SUBLANES, qw), F32)],
        compiler_params=_params(("arbitrary", "arbitrary")),
        name="mlstm",
    )(proj, proj, proj, proj, proj, proj, conv_w, conv_w, gate_b, norm_g, tri)


def _rwkv_kernel(r_ref, k_ref, v_ref, z_ref, lr_ref, mur_ref, muk_ref, muv_ref, mulr_ref, w0_ref, a0_ref,
                 kk_ref, ka_ref, rk_ref, lng_ref, lnb_ref, w2h_ref, w2l_ref, tri_ref, ones_ref,
                 o_ref, st_ref, tr_ref, tk_ref, tv_ref, tlr_ref, *, chunks):
    npairs = BRANCH_WIDTH // LANES
    pairs = range(npairs)

    @pl.when(pl.program_id(1) == 0)
    def _():
        st_ref[...] = jnp.zeros_like(st_ref)
        tr_ref[...] = jnp.zeros_like(tr_ref)
        tk_ref[...] = jnp.zeros_like(tk_ref)
        tv_ref[...] = jnp.zeros_like(tv_ref)
        tlr_ref[...] = jnp.zeros_like(tlr_ref)

    def lerp(x_ref, tail_ref, mu_ref, rows):
        x = x_ref[rows, :]
        out = x + (_shift_rows(x, tail_ref[...], 1) - x) * mu_ref[...]
        tail_ref[...] = x[CHUNK - SUBLANES:, :]
        return out

    def chunk(c, carry):
        rows = pl.ds(pl.multiple_of(c * CHUNK, CHUNK), CHUNK)
        tri = tri_ref[...]
        ones_bd = ones_ref[...]
        lane = lax.broadcasted_iota(jnp.int32, (CHUNK, LANES), 1)
        t_i = lax.broadcasted_iota(jnp.int32, (CHUNK, LANES), 0)
        s_i = lane % C_DH
        head0 = lane < C_DH
        strict = s_i < t_i
        incl = s_i <= t_i
        bd_r = lax.broadcasted_iota(jnp.int32, (LANES, LANES), 0) // C_DH
        bd_c = lax.broadcasted_iota(jnp.int32, (LANES, LANES), 1) // C_DH
        blockdiag = bd_r == bd_c

        def stack(x):
            return jnp.concatenate([jnp.where(head0, x, 0.0), jnp.where(head0, 0.0, x)], axis=0)

        def gsum(x):
            hi = x.astype(BF16)
            lo = (x - hi.astype(F32)).astype(BF16)
            res = _dot(jnp.concatenate(_tiles(hi) + _tiles(lo), axis=0), ones_bd)
            half = npairs * CHUNK
            res = res[:half] + res[half:]
            return jnp.concatenate([res[p * CHUNK:(p + 1) * CHUNK] for p in pairs], axis=1)

        lr = lerp(lr_ref, tlr_ref, mulr_ref, rows)
        lr = jnp.where(head0, jnp.tanh(lr), lr)
        lr_hi = lr.astype(BF16)
        lr_lo = (lr - lr_hi.astype(F32)).astype(BF16)
        up2 = _dot(jnp.concatenate([lr_hi, lr_lo], axis=0), w2h_ref[...])
        up = up2[:CHUNK] + up2[CHUNK:] + _dot(lr_hi, w2l_ref[...])

        r = lerp(r_ref, tr_ref, mur_ref, rows)
        k0 = lerp(k_ref, tk_ref, muk_ref, rows)
        v = lerp(v_ref, tv_ref, muv_ref, rows)
        w_log = -_softplus(-(w0_ref[...] + up[:, :BRANCH_WIDTH])) - 0.5
        lw = -jnp.exp(w_log)
        a = _sigmoid(a0_ref[...] + up[:, BRANCH_WIDTH:])
        kk = k0 * kk_ref[...]
        kk = kk / jnp.maximum(jnp.sqrt(gsum(kk * kk)), 1e-12)
        k = k0 * (1.0 + (a - 1.0) * ka_ref[...])
        b = kk * a

        cw = _cumsum_rows(tri, lw)
        e_c = jnp.exp(cw)
        e_n = jnp.exp(-cw)
        at = _tiles(-kk * jnp.exp(cw - lw))
        rt = _tiles(r * e_c)
        bt = _tiles(b * e_n)
        kt = _tiles(k * e_n)
        vp = _tiles(v)
        decay = _tiles(e_c[CHUNK - 1:CHUNK, :])
        st = [st_ref[p] for p in pairs]

        ar = [jnp.concatenate([at[p], rt[p]], axis=0).astype(BF16) for p in pairs]
        bk = [jnp.concatenate([stack(bt[p]), stack(kt[p])], axis=0).astype(BF16) for p in pairs]
        pm = [_dot_nt(ar[p], bk[p]) for p in pairs]
        a_s = [_dot_nt(ar[p], st[p].astype(BF16)) for p in pairs]
        v_st = [stack(vp[p]).astype(BF16) for p in pairs]
        l_ak = [jnp.where(strict, pm[p][:CHUNK, LANES:], 0.0).astype(BF16) for p in pairs]
        u = [a_s[p][:CHUNK] + _dot(l_ak[p], v_st[p]) for p in pairs]
        mk = [jnp.where(strict, pm[p][:CHUNK, :LANES], 0.0) for p in pairs]
        steps = CHUNK.bit_length() - 1
        for it in range(steps - 1):
            res = [_dot(mk[p].astype(BF16),
                        jnp.concatenate([stack(u[p]), stack(mk[p])], axis=1).astype(BF16)) for p in pairs]
            u = [u[p] + res[p][:, :LANES] for p in pairs]
            mk = [res[p][:, LANES:] for p in pairs]
        u = [u[p] + _dot(mk[p].astype(BF16), stack(u[p]).astype(BF16)) for p in pairs]

        rbk = [jnp.concatenate([jnp.where(incl, pm[p][CHUNK:, :LANES], 0.0),
                                jnp.where(incl, pm[p][CHUNK:, LANES:], 0.0)], axis=1).astype(BF16) for p in pairs]
        y = [a_s[p][CHUNK:] + _dot(rbk[p], jnp.concatenate([stack(u[p]).astype(BF16), v_st[p]], axis=0))
             for p in pairs]
        uvt = [jnp.concatenate([u[p], vp[p]], axis=0).T.astype(BF16) for p in pairs]
        bkt = [jnp.concatenate([bt[p], kt[p]], axis=0).astype(BF16) for p in pairs]
        for p in pairs:
            st_ref[p] = (st[p] + jnp.where(blockdiag, _dot(uvt[p], bkt[p]), 0.0)) * decay[p]

        y = jnp.concatenate(y, axis=1)
        yc = y - gsum(y) * (1.0 / C_DH)
        var = gsum(yc * yc) * (1.0 / C_DH)
        yn = yc * lax.rsqrt(var + C_LN_EPS) * lng_ref[...] + lnb_ref[...]
        bonus = gsum(r * k * rk_ref[...]) * v
        o_ref[rows, :] = ((yn + bonus) * _silu(z_ref[rows, :])).astype(o_ref.dtype)
        return carry

    lax.fori_loop(0, chunks, chunk, 0)


def _rwkv(proj, rows_cw, mu_lr, w2h, w2l, tri, ones_bd, bsz, t_pad, tb):
    m = proj.shape[0]
    nt = t_pad // tb
    w = BRANCH_WIDTH

    def col(off, width):
        return pl.BlockSpec((tb, width), lambda b, t: (b * nt + t, off // width))

    def whole(shape):
        return pl.BlockSpec(shape, lambda b, t: (0, 0))

    row = whole((1, w))
    return pl.pallas_call(
        functools.partial(_rwkv_kernel, chunks=tb // CHUNK),
        grid=(bsz, nt),
        in_specs=[col(C_R, w), col(C_K, w), col(C_V, w), col(C_Z, w), col(C_LR, LANES),
                  row, row, row, whole((1, LANES)), row, row, row, row, row, row, row,
                  whole((LANES, 2 * w)), whole((LANES, 2 * w)), whole((CHUNK, CHUNK)), whole((LANES, LANES))],
        out_specs=pl.BlockSpec((tb, w), lambda b, t: (b * nt + t, 0)),
        out_shape=jax.ShapeDtypeStruct((m, w), BF16),
        scratch_shapes=[pltpu.VMEM((w // LANES, LANES, LANES), F32),
                        pltpu.VMEM((SUBLANES, w), F32),
                        pltpu.VMEM((SUBLANES, w), F32),
                        pltpu.VMEM((SUBLANES, w), F32),
                        pltpu.VMEM((SUBLANES, LANES), F32)],
        compiler_params=_params(("arbitrary", "arbitrary")),
        name="rwkv7",
    )(proj, proj, proj, proj, proj, *rows_cw[:3], mu_lr, *rows_cw[3:], w2h, w2l, tri, ones_bd)


def _post_kernel(ya_ref, yb_ref, yc_ref, ga_ref, gb_ref, gc_ref, wbr_ref, wout_ref, h_ref, gn_ref,
                 o_ref, xn_ref, mg_ref, hrow_ref, *, nsub, tn):
    j = pl.program_id(2)

    @pl.when(j < nsub)
    def _():
        acc = _sigmoid(ga_ref[...]) * _dot(ya_ref[...], wbr_ref[0])
        acc = acc + _sigmoid(gb_ref[...]) * _dot(yb_ref[...], wbr_ref[1])
        acc = acc + _sigmoid(gc_ref[...]) * _dot(yc_ref[...], wbr_ref[2])
        mg_ref[j] = acc.astype(BF16)

    @pl.when(j >= nsub)
    def _():
        new = h_ref[...]
        for kk in range(nsub):
            new = new + _dot(mg_ref[kk], wout_ref[kk * tn:(kk + 1) * tn, :])
        row = lax.broadcasted_iota(jnp.int32, new.shape, 0)
        keep = jnp.logical_or(pl.program_id(1) > 0, row >= PAD_FRONT)
        new = jnp.where(keep, new, 0.0)
        o_ref[...] = new
        hrow_ref[j - nsub] = new

    @pl.when(j == 2 * nsub - 1)
    def _():
        ss = jnp.sum(hrow_ref[0] * hrow_ref[0], axis=-1, keepdims=True)
        for kk in range(1, nsub):
            ss = ss + jnp.sum(hrow_ref[kk] * hrow_ref[kk], axis=-1, keepdims=True)
        scale = lax.rsqrt(ss * (1.0 / (nsub * tn)) + NORM_EPS)
        for kk in range(nsub):
            cols = slice(kk * tn, (kk + 1) * tn)
            xn_ref[:, cols] = (hrow_ref[kk] * scale * gn_ref[:, cols]).astype(xn_ref.dtype)


def _post(ya, yb, yc, proj, w_br, w_out, h, g_next, layer, bsz, t_pad, tm, tn):
    m, d = h.shape
    nt = t_pad // tm
    nsub = d // tn
    y_spec = pl.BlockSpec((tm, BRANCH_WIDTH), lambda b, t, j: (b * nt + t, 0))

    def gate(off):
        return pl.BlockSpec((tm, tn), lambda b, t, j: (b * nt + t, off // tn + jnp.minimum(j, nsub - 1)))

    def out_col(j):
        return jnp.maximum(j - nsub, 0)

    return pl.pallas_call(
        functools.partial(_post_kernel, nsub=nsub, tn=tn),
        grid=(bsz, nt, 2 * nsub),
        in_specs=[y_spec, y_spec, y_spec, gate(G_A), gate(G_B), gate(G_C),
                  pl.BlockSpec((None, 3, BRANCH_WIDTH, tn), lambda b, t, j: (layer, 0, 0, jnp.minimum(j, nsub - 1))),
                  pl.BlockSpec((None, d, tn), lambda b, t, j: (layer, 0, out_col(j))),
                  pl.BlockSpec((tm, tn), lambda b, t, j: (b * nt + t, out_col(j))),
                  pl.BlockSpec((1, d), lambda b, t, j: (0, 0))],
        out_specs=[pl.BlockSpec((tm, tn), lambda b, t, j: (b * nt + t, out_col(j))),
                   pl.BlockSpec((tm, d), lambda b, t, j: (b * nt + t, 0))],
        out_shape=[jax.ShapeDtypeStruct((m, d), F32), jax.ShapeDtypeStruct((m, d), BF16)],
        scratch_shapes=[pltpu.VMEM((nsub, tm, tn), BF16), pltpu.VMEM((nsub, tm, tn), F32)],
        input_output_aliases={8: 0},
        compiler_params=_params(("arbitrary", "arbitrary", "arbitrary")),
        name="post",
    )(ya, yb, yc, proj, proj, proj, w_br, w_out, h, g_next)


SRC_B_G = B_O + BRANCH_WIDTH
SRC_C_LR = SRC_B_G + 2 * B_HEADS + 4 * BRANCH_WIDTH
N_IN = SRC_C_LR + LANES + BRANCH_WIDTH + 3 * 2048
COPY_RUNS = ((0, 0, SRC_B_G),
             (B_Z, SRC_B_G + 2 * B_HEADS, 4 * BRANCH_WIDTH),
             (C_Z, SRC_C_LR + LANES, BRANCH_WIDTH + 3 * 2048))
REGROUP_ROWS = 64
REGROUP_STEP = 512


def _regroup_kernel(w_ref, o_ref):
    def shifted(src, width):
        base = src // LANES * LANES
        end = min(base + width + LANES, N_IN) if src != base else src + width
        return w_ref[:, base:end][:, src - base:src - base + width]

    for dst, src, width in COPY_RUNS:
        for c in range(0, width, REGROUP_STEP):
            o_ref[:, dst + c:dst + c + REGROUP_STEP] = shifted(src + c, REGROUP_STEP).astype(BF16)
    lane = lax.broadcasted_iota(jnp.int32, (REGROUP_ROWS, LANES), 1)
    gates = w_ref[:, SRC_B_G:SRC_B_G + LANES]
    o_ref[:, B_G:B_G + LANES] = jnp.where(lane < 2 * B_HEADS, gates, 0.0).astype(BF16)
    o_ref[:, C_LR:C_LR + LANES] = shifted(SRC_C_LR, LANES).astype(BF16)
    o_ref[:, C_LR + LANES:] = jnp.zeros((REGROUP_ROWS, N_PROJ - C_LR - LANES), BF16)


def _regroup_w_in(w_in):
    depth, d, n_in = w_in.shape
    assert n_in == N_IN and d % REGROUP_ROWS == 0
    return pl.pallas_call(
        _regroup_kernel,
        grid=(depth, d // REGROUP_ROWS),
        in_specs=[pl.BlockSpec((None, REGROUP_ROWS, n_in), lambda l, i: (l, i, 0))],
        out_specs=pl.BlockSpec((None, REGROUP_ROWS, N_PROJ), lambda l, i: (l, i, 0)),
        out_shape=jax.ShapeDtypeStruct((depth, d, N_PROJ), BF16),
        compiler_params=_params(("arbitrary", "arbitrary")),
        name="regroup_w_in",
    )(w_in)


def _rwkv_up_weights(w_up, a_up):
    top = jnp.concatenate([w_up, jnp.zeros_like(w_up)], axis=-1)
    bot = jnp.concatenate([jnp.zeros_like(a_up), a_up], axis=-1)
    w2 = jnp.concatenate([top, bot], axis=1)
    hi = w2.astype(BF16)
    lo = (w2 - hi.astype(F32)).astype(BF16)
    return hi, lo


def kernel(x, meta_tokens, norm_g, w_in, hgrn_lb_logits, hgrn_norm_g, mlstm_conv, mlstm_ig_b, mlstm_fg_b,
           mlstm_norm_g, rwkv_mu, rwkv_w0, rwkv_w_up, rwkv_a0, rwkv_a_up, rwkv_k_k, rwkv_k_a, rwkv_r_k,
           rwkv_ln_g, rwkv_ln_b, w_br, w_out, final_norm_g):
    bsz, seq, d = x.shape
    depth = w_in.shape[0]
    t_pad = PAD_FRONT + N_META + seq
    m = bsz * t_pad
    assert seq % (2 * CHUNK) == 0 and t_pad % CHUNK == 0

    tb = _divisor_tile(t_pad, 640, CHUNK)
    tm_in = _divisor_tile(m, 1664, LANES)
    tm_row = _divisor_tile(m, 640, LANES)
    tm_post = _divisor_tile(t_pad, 640, LANES)
    assert tm_post >= PAD_FRONT

    w_in_p = _regroup_w_in(w_in)
    w_br_b = w_br.astype(BF16)
    w_out_b = w_out.astype(BF16)
    p = jax.nn.softmax(hgrn_lb_logits.astype(F32), axis=0)
    lower_bounds = jnp.cumsum(p, axis=0) - p[0]
    gate_b = jnp.concatenate([mlstm_ig_b, mlstm_fg_b, jnp.zeros((depth, LANES - 2 * B_HEADS), F32)], axis=-1)
    w2h, w2l = _rwkv_up_weights(rwkv_w_up, rwkv_a_up)
    w = BRANCH_WIDTH
    mu_r, mu_k, mu_v, mu_lr = rwkv_mu[:, :w], rwkv_mu[:, w:2 * w], rwkv_mu[:, 2 * w:3 * w], rwkv_mu[:, 3 * w:]
    tri = jnp.asarray(np.tril(np.ones((CHUNK, CHUNK), np.float32)), BF16)
    levels = jnp.asarray(_hgrn2_levels())
    lane_head = np.arange(LANES) // C_DH
    ones_bd = jnp.asarray((lane_head[:, None] == lane_head[None, :]).astype(np.float32), BF16)

    meta = jnp.broadcast_to(meta_tokens[None].astype(F32), (bsz, N_META, d))
    h = jnp.concatenate([jnp.zeros((bsz, PAD_FRONT, d), F32), meta, x.astype(F32)], axis=1).reshape(m, d)

    xn = _rmsnorm(h, norm_g[0][None, :], BF16, tm_row)
    for l in range(depth):
        row = lambda a: a[l][None, :]
        proj = _inproj(xn, w_in_p, l, tm_in, 512)
        ya = _hgrn2(proj, row(lower_bounds), row(hgrn_norm_g), tri, levels, bsz, t_pad, tb)
        yb = _mlstm(proj, mlstm_conv[l], row(gate_b), row(mlstm_norm_g), tri, bsz, t_pad, tb)
        rows_cw = [row(a) for a in (mu_r, mu_k, mu_v, rwkv_w0, rwkv_a0, rwkv_k_k, rwkv_k_a, rwkv_r_k,
                                    rwkv_ln_g, rwkv_ln_b)]
        yc = _rwkv(proj, rows_cw, row(mu_lr), w2h[l], w2l[l], tri, ones_bd, bsz, t_pad, tb)
        g_next = norm_g[min(l + 1, depth - 1)][None, :]
        h, xn = _post(ya, yb, yc, proj, w_br_b, w_out_b, h, g_next, l, bsz, t_pad, tm_post, 512)

    return _final_norm(h, final_norm_g[None, :], bsz, t_pad, seq)
```

```python
import functools

import numpy as np
import jax
import jax.numpy as jnp
from jax import lax
from jax.experimental import pallas as pl
from jax.experimental.pallas import tpu as pltpu

F32 = jnp.float32
BF16 = jnp.bfloat16

N_META = 16
CHUNK = 64
NORM_EPS = 1e-6
NEG_BIG = -1e30
F_FLOOR = 1e-12
BRANCH_WIDTH = 1024
A_HEADS, A_DK = 8, 128
B_HEADS, B_DQK, B_DV, B_CONV = 4, 128, 256, 4
C_HEADS, C_DH, C_RANK = 16, 64, 64
C_LN_EPS = 64e-5

LANES = 128
SUBLANES = 8
PAD_FRONT = 2 * CHUNK - N_META
VMEM_LIMIT = 56 * 1024 * 1024

A_Q, A_F, A_I, A_Z = 0, 1024, 2048, 3072
B_Q, B_K, B_V, B_O, B_Z = 4096, 4608, 5120, 6144, 7168
C_R, C_K, C_V, C_Z = 8192, 9216, 10240, 11264
G_A, G_B, G_C = 12288, 14336, 16384
B_G = 18432
C_LR = 18560
N_PROJ = 18944


def _divisor_tile(n, target, mult):
    best = None
    for d in range(mult, min(n, target) + 1, mult):
        if n % d == 0:
            best = d
    if best is None:
        raise ValueError(f"no tile for {n} (multiple of {mult}, <= {target})")
    return best


def _params(sem):
    return pltpu.CompilerParams(dimension_semantics=sem, vmem_limit_bytes=VMEM_LIMIT)


def _sigmoid(x):
    return 0.5 * jnp.tanh(0.5 * x) + 0.5


def _silu(x):
    return x * _sigmoid(x)


def _softplus(x):
    return jnp.maximum(x, 0.0) + jnp.log(1.0 + jnp.exp(-jnp.abs(x)))


def _dot(a, b):
    return jnp.dot(a, b, preferred_element_type=F32)


def _dot_nt(a, b):
    return lax.dot_general(a, b, (((1,), (1,)), ((), ())), preferred_element_type=F32)


def _split3(x):
    hi = x.astype(BF16)
    r = x - hi.astype(F32)
    mid = r.astype(BF16)
    lo = (r - mid.astype(F32)).astype(BF16)
    return hi, mid, lo


def _cumsum_rows(tri, x):
    w = x.shape[1]
    hi, mid, lo = _split3(x)
    r = _dot(tri, jnp.concatenate([hi, mid, lo], axis=1))
    return (r[:, 2 * w:] + r[:, w:2 * w]) + r[:, :w]


def _row_bcast(x, r, n):
    return jnp.broadcast_to(x[r:r + 1, :], (n, x.shape[1]))


def _shift_rows(x, tail, j):
    rolled = pltpu.roll(x, j, 0)
    tail_r = pltpu.roll(tail, j, 0)
    row = lax.broadcasted_iota(jnp.int32, tail.shape, 0)
    first = jnp.where(row < j, tail_r, rolled[:SUBLANES])
    return jnp.concatenate([first, rolled[SUBLANES:]], axis=0)


def _tiles(x, w=LANES):
    return [x[:, i * w:(i + 1) * w] for i in range(x.shape[1] // w)]


def _rmsnorm_kernel(x_ref, g_ref, o_ref):
    x = x_ref[...]
    ms = jnp.mean(x * x, axis=-1, keepdims=True)
    o_ref[...] = (x * lax.rsqrt(ms + NORM_EPS) * g_ref[...]).astype(o_ref.dtype)


def _rmsnorm(h, g, out_dtype, tm):
    m, d = h.shape
    return pl.pallas_call(
        _rmsnorm_kernel,
        grid=(m // tm,),
        in_specs=[pl.BlockSpec((tm, d), lambda i: (i, 0)), pl.BlockSpec((1, d), lambda i: (0, 0))],
        out_specs=pl.BlockSpec((tm, d), lambda i: (i, 0)),
        out_shape=jax.ShapeDtypeStruct((m, d), out_dtype),
        compiler_params=_params(("arbitrary",)),
        name="rmsnorm",
    )(h, g)


def _final_norm_kernel(x_ref, g_ref, o_ref):
    x = x_ref[...]
    ms = jnp.mean(x * x, axis=-1, keepdims=True)
    o_ref[0] = x * lax.rsqrt(ms + NORM_EPS) * g_ref[...]


def _final_norm(h, g, bsz, t_pad, seq):
    d = h.shape[1]
    tm = CHUNK * 2
    per_b = t_pad // tm
    return pl.pallas_call(
        _final_norm_kernel,
        grid=(bsz, seq // tm),
        in_specs=[pl.BlockSpec((tm, d), lambda b, i: (b * per_b + i + 1, 0)),
                  pl.BlockSpec((1, d), lambda b, i: (0, 0))],
        out_specs=pl.BlockSpec((1, tm, d), lambda b, i: (b, i, 0)),
        out_shape=jax.ShapeDtypeStruct((bsz, seq, d), F32),
        compiler_params=_params(("arbitrary", "arbitrary")),
        name="final_norm",
    )(h, g)


def _inproj_kernel(x_ref, w_ref, o_ref):
    o_ref[...] = _dot(x_ref[...], w_ref[...])


def _inproj(xn, w, layer, tm, tn):
    m, d = xn.shape
    n = w.shape[2]
    return pl.pallas_call(
        _inproj_kernel,
        grid=(m // tm, n // tn),
        in_specs=[pl.BlockSpec((tm, d), lambda i, j: (i, 0)),
                  pl.BlockSpec((None, d, tn), lambda i, j: (layer, 0, j))],
        out_specs=pl.BlockSpec((tm, tn), lambda i, j: (i, j)),
        out_shape=jax.ShapeDtypeStruct((m, n), F32),
        compiler_params=_params(("arbitrary", "arbitrary")),
        name="inproj",
    )(xn, w)


def _hgrn2_levels():
    t = np.arange(CHUNK)[:, None]
    s = np.arange(CHUNK)[None, :]
    lv = np.zeros((CHUNK, CHUNK), np.int32)
    lv[(t // 4 == s // 4) & (s <= t)] = 1
    for lid, sz in ((2, 4), (3, 8), (4, 16), (5, 32)):
        lv[(t // (2 * sz) == s // (2 * sz)) & (t % (2 * sz) >= sz) & (s % (2 * sz) < sz)] = lid
    return lv


def _hgrn2_chunk(rows, q_ref, f_ref, i_ref, z_ref, lb_ref, g_ref, tri_ref, lv_ref, o_ref, st_ref):
    heads = range(A_HEADS)
    lv = lv_ref[...]
    sub = lax.broadcasted_iota(jnp.int32, (SUBLANES, BRANCH_WIDTH), 0)
    lb = lb_ref[...]
    fr = f_ref[rows, :]
    v = i_ref[rows, :]
    qs = _silu(q_ref[rows, :]) * (A_DK ** -0.5)
    sg = _sigmoid(fr)
    k = (1.0 - lb) * (1.0 - sg)
    lf = jnp.log(jnp.maximum(lb + (1.0 - lb) * sg, F_FLOOR))
    cg = _cumsum_rows(tri_ref[...], lf)
    yield
    cl = cg[CHUNK - 1:CHUNK, :]
    st = [st_ref[j] for j in heads]
    qg = _tiles((qs * jnp.exp(cg)).astype(BF16))
    o = [_dot_nt(qg[j], st[j].astype(BF16)) for j in heads]
    yield

    ref = jnp.concatenate(
        [jnp.where(sub < 4, _row_bcast(cg, 8 * g, SUBLANES), _row_bcast(cg, 8 * g + 4, SUBLANES))
         for g in range(CHUNK // SUBLANES)], axis=0)
    qt = _tiles((qs * jnp.exp(cg - ref)).astype(BF16))
    kt = _tiles((k * jnp.exp(ref - cg)).astype(BF16))
    att = [jnp.where(lv == 1, _dot_nt(qt[j], kt[j]), 0.0) for j in heads]
    yield
    row_i = lax.broadcasted_iota(jnp.int32, (CHUNK, BRANCH_WIDTH), 0)
    for lid, sz in ((2, 4), (3, 8), (4, 16), (5, 32)):
        ref = jnp.concatenate(
            [_row_bcast(cg, g * 2 * sz + sz - 1, 2 * sz) for g in range(CHUNK // (2 * sz))], axis=0)
        later = (row_i & sz) != 0
        x = _tiles((jnp.where(later, qs, k) * jnp.exp(-jnp.abs(cg - ref))).astype(BF16))
        att = [jnp.where(lv == lid, _dot_nt(x[j], x[j]), att[j]) for j in heads]
        yield

    vb = _tiles(v.astype(BF16))
    o = [o[j] + _dot(att[j].astype(BF16), vb[j]) for j in heads]
    yield
    kd = _tiles((k * jnp.exp(cl - cg)).astype(BF16))
    vt = [t.T.astype(BF16) for t in _tiles(v)]
    decay = _tiles(jnp.exp(cl))
    for j in heads:
        st_ref[j] = st[j] * decay[j] + _dot(vt[j], kd[j])
    yield
    on = [o[j] * lax.rsqrt(jnp.mean(o[j] * o[j], axis=-1, keepdims=True) + NORM_EPS) for j in heads]
    y = jnp.concatenate(on, axis=1) * g_ref[...] * _silu(z_ref[rows, :])
    o_ref[rows, :] = y.astype(o_ref.dtype)


def _mlstm_chunk(rows, q_ref, k_ref, v_ref, og_ref, z_ref, gt_ref, cwq_ref, cwk_ref, gb_ref, ng_ref, tri_ref,
                 o_ref, c_ref, m_ref, tq_ref, tk_ref):
    heads = range(B_HEADS)

    def conv(x, tail, w):
        acc = x * w[B_CONV - 1:B_CONV, :]
        for j in range(1, B_CONV):
            acc = acc + _shift_rows(x, tail, j) * w[B_CONV - 1 - j:B_CONV - j, :]
        return _silu(acc)

    t_i = lax.broadcasted_iota(jnp.int32, (CHUNK, CHUNK), 0)
    s_i = lax.broadcasted_iota(jnp.int32, (CHUNK, CHUNK), 1)
    causal = s_i <= t_i
    lane = lax.broadcasted_iota(jnp.int32, (CHUNK, LANES), 1)
    sub = lax.broadcasted_iota(jnp.int32, (LANES, CHUNK), 0)

    gates = gt_ref[rows, :] + gb_ref[...]
    lf = jnp.minimum(gates, 0.0) - jnp.log(1.0 + jnp.exp(-jnp.abs(gates)))
    bcum = _cumsum_rows(tri_ref[...], lf)
    yield
    qraw = q_ref[rows, :]
    kraw = k_ref[rows, :]
    qc = _tiles((conv(qraw, tq_ref[...], cwq_ref[...]) * (B_DQK ** -0.5)).astype(BF16), B_DQK)
    kc32 = _tiles(conv(kraw, tk_ref[...], cwk_ref[...]), B_DQK)
    kc = [t.astype(BF16) for t in kc32]
    tq_ref[...] = qraw[CHUNK - SUBLANES:, :]
    tk_ref[...] = kraw[CHUNK - SUBLANES:, :]
    c_aug = [c_ref[j] for j in heads]
    qk = [_dot_nt(qc[j], kc[j]) for j in heads]
    qcs = [_dot(qc[j], c_aug[j].astype(BF16)) for j in heads]
    yield

    gates_t = gates.T
    bcum_t = bcum.T
    ig_col = [jnp.sum(jnp.where(lane == j, gates, 0.0), axis=1, keepdims=True) for j in heads]
    b_col = [jnp.sum(jnp.where(lane == B_HEADS + j, bcum, 0.0), axis=1, keepdims=True) for j in heads]
    ig_row = [jnp.sum(jnp.where(sub == j, gates_t, 0.0), axis=0, keepdims=True) for j in heads]
    b_row = [jnp.sum(jnp.where(sub == B_HEADS + j, bcum_t, 0.0), axis=0, keepdims=True) for j in heads]
    yield

    ones = jnp.ones((CHUNK, LANES), BF16)
    v_aug = [jnp.concatenate([t.astype(BF16), ones], axis=1) for t in _tiles(v_ref[rows, :], B_DV)]
    m_prev = [m_ref[j][0:1, 0:1] for j in heads]
    log_w = [jnp.where(causal, b_col[j] - b_row[j] + ig_row[j], NEG_BIG) for j in heads]
    log_inter = [b_col[j] + m_prev[j] for j in heads]
    m_t = [jnp.maximum(log_inter[j], jnp.max(log_w[j], axis=-1, keepdims=True)) for j in heads]
    scores = [(qk[j] * jnp.exp(log_w[j] - m_t[j])).astype(BF16) for j in heads]
    numden = [_dot(scores[j], v_aug[j]) + jnp.exp(log_inter[j] - m_t[j]) * qcs[j] for j in heads]
    yield

    b_end = [b_col[j][CHUNK - 1:CHUNK, :] for j in heads]
    log_s = [b_end[j] - b_col[j] + ig_col[j] for j in heads]
    m_new = [jnp.maximum(b_end[j] + m_prev[j], jnp.max(log_s[j], axis=0, keepdims=True)) for j in heads]
    kw = [(kc32[j] * jnp.exp(log_s[j] - m_new[j])).T.astype(BF16) for j in heads]
    for j in heads:
        c_ref[j] = jnp.exp(b_end[j] + m_prev[j] - m_new[j]) * c_aug[j] + _dot(kw[j], v_aug[j])
        m_ref[j] = jnp.broadcast_to(m_new[j], (SUBLANES, LANES))
    yield

    hid = [numden[j][:, :B_DV] / jnp.maximum(jnp.abs(numden[j][:, B_DV:B_DV + 1]), jnp.exp(-m_t[j]))
           for j in heads]
    hc = [hid[j] - jnp.mean(hid[j], axis=-1, keepdims=True) for j in heads]
    hn = [hc[j] * lax.rsqrt(jnp.mean(hc[j] * hc[j], axis=-1, keepdims=True) + NORM_EPS) for j in heads]
    y = jnp.concatenate(hn, axis=1) * ng_ref[...] * _sigmoid(og_ref[rows, :]) * _silu(z_ref[rows, :])
    o_ref[rows, :] = y.astype(o_ref.dtype)


def _rwkv_chunk(rows, r_ref, k_ref, v_ref, z_ref, lr_ref, mur_ref, muk_ref, muv_ref, mulr_ref, w0_ref, a0_ref,
                kk_ref, ka_ref, rk_ref, lng_ref, lnb_ref, w2h_ref, w2l_ref, tri_ref, ones_ref,
                o_ref, st_ref, tr_ref, tk_ref, tv_ref, tlr_ref):
    npairs = BRANCH_WIDTH // LANES
    pairs = range(npairs)

    def lerp(x_ref, tail_ref, mu_ref):
        x = x_ref[rows, :]
        out = x + (_shift_rows(x, tail_ref[...], 1) - x) * mu_ref[...]
        tail_ref[...] = x[CHUNK - SUBLANES:, :]
        return out

    tri = tri_ref[...]
    ones_bd = ones_ref[...]
    lane = lax.broadcasted_iota(jnp.int32, (CHUNK, LANES), 1)
    t_i = lax.broadcasted_iota(jnp.int32, (CHUNK, LANES), 0)
    s_i = lane % C_DH
    head0 = lane < C_DH
    strict = s_i < t_i
    incl = s_i <= t_i
    bd_r = lax.broadcasted_iota(jnp.int32, (LANES, LANES), 0) // C_DH
    bd_c = lax.broadcasted_iota(jnp.int32, (LANES, LANES), 1) // C_DH
    blockdiag = bd_r == bd_c

    def stack(x):
        return jnp.concatenate([jnp.where(head0, x, 0.0), jnp.where(head0, 0.0, x)], axis=0)

    def gsum(x):
        hi = x.astype(BF16)
        lo = (x - hi.astype(F32)).astype(BF16)
        res = _dot(jnp.concatenate(_tiles(hi) + _tiles(lo), axis=0), ones_bd)
        half = npairs * CHUNK
        res = res[:half] + res[half:]
        return jnp.concatenate([res[p * CHUNK:(p + 1) * CHUNK] for p in pairs], axis=1)

    lr = lerp(lr_ref, tlr_ref, mulr_ref)
    lr = jnp.where(head0, jnp.tanh(lr), lr)
    lr_hi = lr.astype(BF16)
    lr_lo = (lr - lr_hi.astype(F32)).astype(BF16)
    up2 = _dot(jnp.concatenate([lr_hi, lr_lo], axis=0), w2h_ref[...])
    up_lo = _dot(lr_hi, w2l_ref[...])
    k0 = lerp(k_ref, tk_ref, muk_ref)
    kk = k0 * kk_ref[...]
    kk_ss = gsum(kk * kk)
    yield
    up = up2[:CHUNK] + up2[CHUNK:] + up_lo
    w_log = -_softplus(-(w0_ref[...] + up[:, :BRANCH_WIDTH])) - 0.5
    lw = -jnp.exp(w_log)
    cw = _cumsum_rows(tri, lw)
    yield
    r = lerp(r_ref, tr_ref, mur_ref)
    v = lerp(v_ref, tv_ref, muv_ref)
    a = _sigmoid(a0_ref[...] + up[:, BRANCH_WIDTH:])
    kk = kk / jnp.maximum(jnp.sqrt(kk_ss), 1e-12)
    k = k0 * (1.0 + (a - 1.0) * ka_ref[...])
    b = kk * a
    bonus_s = gsum(r * k * rk_ref[...])
    e_c = jnp.exp(cw)
    e_n = jnp.exp(-cw)
    at = _tiles(-kk * jnp.exp(cw - lw))
    rt = _tiles(r * e_c)
    bt = _tiles(b * e_n)
    kt = _tiles(k * e_n)
    vp = _tiles(v)
    decay = _tiles(e_c[CHUNK - 1:CHUNK, :])
    st = [st_ref[p] for p in pairs]

    ar = [jnp.concatenate([at[p], rt[p]], axis=0).astype(BF16) for p in pairs]
    bk = [jnp.concatenate([stack(bt[p]), stack(kt[p])], axis=0).astype(BF16) for p in pairs]
    pm = [_dot_nt(ar[p], bk[p]) for p in pairs]
    a_s = [_dot_nt(ar[p], st[p].astype(BF16)) for p in pairs]
    yield
    v_st = [stack(vp[p]).astype(BF16) for p in pairs]
    l_ak = [jnp.where(strict, pm[p][:CHUNK, LANES:], 0.0).astype(BF16) for p in pairs]
    u = [a_s[p][:CHUNK] + _dot(l_ak[p], v_st[p]) for p in pairs]
    yield
    mk = [jnp.where(strict, pm[p][:CHUNK, :LANES], 0.0) for p in pairs]
    steps = CHUNK.bit_length() - 1
    for it in range(steps - 1):
        res = [_dot(mk[p].astype(BF16),
                    jnp.concatenate([stack(u[p]), stack(mk[p])], axis=1).astype(BF16)) for p in pairs]
        yield
        u = [u[p] + res[p][:, :LANES] for p in pairs]
        mk = [res[p][:, LANES:] for p in pairs]
    u = [u[p] + _dot(mk[p].astype(BF16), stack(u[p]).astype(BF16)) for p in pairs]
    yield

    rbk = [jnp.concatenate([jnp.where(incl, pm[p][CHUNK:, :LANES], 0.0),
                            jnp.where(incl, pm[p][CHUNK:, LANES:], 0.0)], axis=1).astype(BF16) for p in pairs]
    y = [a_s[p][CHUNK:] + _dot(rbk[p], jnp.concatenate([stack(u[p]).astype(BF16), v_st[p]], axis=0))
         for p in pairs]
    uvt = [jnp.concatenate([u[p], vp[p]], axis=0).T.astype(BF16) for p in pairs]
    bkt = [jnp.concatenate([bt[p], kt[p]], axis=0).astype(BF16) for p in pairs]
    for p in pairs:
        st_ref[p] = (st[p] + jnp.where(blockdiag, _dot(uvt[p], bkt[p]), 0.0)) * decay[p]
    yield

    y = jnp.concatenate(y, axis=1)
    mean = gsum(y) * (1.0 / C_DH)
    yield
    yc = y - mean
    var = gsum(yc * yc) * (1.0 / C_DH)
    yield
    yn = yc * lax.rsqrt(var + C_LN_EPS) * lng_ref[...] + lnb_ref[...]
    o_ref[rows, :] = ((yn + bonus_s * v) * _silu(z_ref[rows, :])).astype(o_ref.dtype)


N_A_IN, N_B_IN, N_C_IN = 8, 11, 20
N_A_SCR, N_B_SCR, N_C_SCR = 1, 4, 5


def _mixers_kernel(*refs, chunks):
    refs = list(refs)

    def take(n):
        out = refs[:n]
        del refs[:n]
        return out

    a_in, b_in, c_in = take(N_A_IN), take(N_B_IN), take(N_C_IN)
    ya_ref, yb_ref, yc_ref = take(3)
    a_scr, b_scr, c_scr = take(N_A_SCR), take(N_B_SCR), take(N_C_SCR)

    @pl.when(pl.program_id(1) == 0)
    def _():
        for ref in a_scr + b_scr + c_scr:
            ref[...] = jnp.zeros_like(ref)

    def chunk(c, carry):
        rows = pl.ds(pl.multiple_of(c * CHUNK, CHUNK), CHUNK)
        live = [_rwkv_chunk(rows, *c_in, yc_ref, *c_scr),
                _hgrn2_chunk(rows, *a_in, ya_ref, *a_scr),
                _mlstm_chunk(rows, *b_in, yb_ref, *b_scr)]
        while live:
            for gen in list(live):
                if next(gen, "done") == "done":
                    live.remove(gen)
        return carry

    lax.fori_loop(0, chunks, chunk, 0)


def _mixers(proj, a_par, b_par, c_par, bsz, t_pad, tb):
    m = proj.shape[0]
    nt = t_pad // tb
    w = BRANCH_WIDTH
    qw = B_HEADS * B_DQK

    def col(off, width):
        return pl.BlockSpec((tb, width), lambda b, t: (b * nt + t, off // width))

    def whole(a):
        return pl.BlockSpec(a.shape, lambda b, t: (0,) * a.ndim)

    a_cols = [col(A_Q, w), col(A_F, w), col(A_I, w), col(A_Z, w)]
    b_cols = [col(B_Q, qw), col(B_K, qw), col(B_V, w), col(B_O, w), col(B_Z, w), col(B_G, LANES)]
    c_cols = [col(C_R, w), col(C_K, w), col(C_V, w), col(C_Z, w), col(C_LR, LANES)]
    conv_w = b_par[0]
    b_specs = [pl.BlockSpec((B_CONV, qw), lambda b, t: (0, 0)),
               pl.BlockSpec((B_CONV, qw), lambda b, t: (0, 1))] + [whole(a) for a in b_par[1:]]
    operands = ([proj] * len(a_cols) + list(a_par) + [proj] * len(b_cols) + [conv_w] + list(b_par)
                + [proj] * len(c_cols) + list(c_par))
    in_specs = (a_cols + [whole(a) for a in a_par] + b_cols + b_specs + c_cols + [whole(a) for a in c_par])
    assert len(a_cols) + len(a_par) == N_A_IN and len(b_cols) + len(b_specs) == N_B_IN
    assert len(c_cols) + len(c_par) == N_C_IN
    out_spec = pl.BlockSpec((tb, w), lambda b, t: (b * nt + t, 0))
    return pl.pallas_call(
        functools.partial(_mixers_kernel, chunks=tb // CHUNK),
        grid=(bsz, nt),
        in_specs=in_specs,
        out_specs=[out_spec] * 3,
        out_shape=[jax.ShapeDtypeStruct((m, w), BF16)] * 3,
        scratch_shapes=[pltpu.VMEM((A_HEADS, LANES, LANES), F32),
                        pltpu.VMEM((B_HEADS, B_DQK, B_DV + LANES), F32),
                        pltpu.VMEM((B_HEADS, SUBLANES, LANES), F32),
                        pltpu.VMEM((SUBLANES, qw), F32),
                        pltpu.VMEM((SUBLANES, qw), F32),
                        pltpu.VMEM((w // LANES, LANES, LANES), F32),
                        pltpu.VMEM((SUBLANES, w), F32),
                        pltpu.VMEM((SUBLANES, w), F32),
                        pltpu.VMEM((SUBLANES, w), F32),
                        pltpu.VMEM((SUBLANES, LANES), F32)],
        compiler_params=_params(("arbitrary", "arbitrary")),
        name="mixers",
    )(*operands)


def _post_kernel(ya_ref, yb_ref, yc_ref, ga_ref, gb_ref, gc_ref, wbr_ref, wout_ref, h_ref, gn_ref,
                 o_ref, xn_ref, mg_ref, hrow_ref, *, nsub, tn):
    j = pl.program_id(2)

    @pl.when(j < nsub)
    def _():
        acc = _sigmoid(ga_ref[...]) * _dot(ya_ref[...], wbr_ref[0])
        acc = acc + _sigmoid(gb_ref[...]) * _dot(yb_ref[...], wbr_ref[1])
        acc = acc + _sigmoid(gc_ref[...]) * _dot(yc_ref[...], wbr_ref[2])
        mg_ref[j] = acc.astype(BF16)

    @pl.when(j >= nsub)
    def _():
        new = h_ref[...]
        for kk in range(nsub):
            new = new + _dot(mg_ref[kk], wout_ref[kk * tn:(kk + 1) * tn, :])
        row = lax.broadcasted_iota(jnp.int32, new.shape, 0)
        keep = jnp.logical_or(pl.program_id(1) > 0, row >= PAD_FRONT)
        new = jnp.where(keep, new, 0.0)
        o_ref[...] = new
        hrow_ref[j - nsub] = new

    @pl.when(j == 2 * nsub - 1)
    def _():
        ss = jnp.sum(hrow_ref[0] * hrow_ref[0], axis=-1, keepdims=True)
        for kk in range(1, nsub):
            ss = ss + jnp.sum(hrow_ref[kk] * hrow_ref[kk], axis=-1, keepdims=True)
        scale = lax.rsqrt(ss * (1.0 / (nsub * tn)) + NORM_EPS)
        for kk in range(nsub):
            cols = slice(kk * tn, (kk + 1) * tn)
            xn_ref[:, cols] = (hrow_ref[kk] * scale * gn_ref[:, cols]).astype(xn_ref.dtype)


def _post(ya, yb, yc, proj, w_br, w_out, h, g_next, layer, bsz, t_pad, tm, tn):
    m, d = h.shape
    nt = t_pad // tm
    nsub = d // tn
    y_spec = pl.BlockSpec((tm, BRANCH_WIDTH), lambda b, t, j: (b * nt + t, 0))

    def gate(off):
        return pl.BlockSpec((tm, tn), lambda b, t, j: (b * nt + t, off // tn + jnp.minimum(j, nsub - 1)))

    def out_col(j):
        return jnp.maximum(j - nsub, 0)

    return pl.pallas_call(
        functools.partial(_post_kernel, nsub=nsub, tn=tn),
        grid=(bsz, nt, 2 * nsub),
        in_specs=[y_spec, y_spec, y_spec, gate(G_A), gate(G_B), gate(G_C),
                  pl.BlockSpec((None, 3, BRANCH_WIDTH, tn), lambda b, t, j: (layer, 0, 0, jnp.minimum(j, nsub - 1))),
                  pl.BlockSpec((None, d, tn), lambda b, t, j: (layer, 0, out_col(j))),
                  pl.BlockSpec((tm, tn), lambda b, t, j: (b * nt + t, out_col(j))),
                  pl.BlockSpec((1, d), lambda b, t, j: (0, 0))],
        out_specs=[pl.BlockSpec((tm, tn), lambda b, t, j: (b * nt + t, out_col(j))),
                   pl.BlockSpec((tm, d), lambda b, t, j: (b * nt + t, 0))],
        out_shape=[jax.ShapeDtypeStruct((m, d), F32), jax.ShapeDtypeStruct((m, d), BF16)],
        scratch_shapes=[pltpu.VMEM((nsub, tm, tn), BF16), pltpu.VMEM((nsub, tm, tn), F32)],
        input_output_aliases={8: 0},
        compiler_params=_params(("arbitrary", "arbitrary", "arbitrary")),
        name="post",
    )(ya, yb, yc, proj, proj, proj, w_br, w_out, h, g_next)


SRC_B_G = B_O + BRANCH_WIDTH
SRC_C_LR = SRC_B_G + 2 * B_HEADS + 4 * BRANCH_WIDTH
N_IN = SRC_C_LR + LANES + BRANCH_WIDTH + 3 * 2048
COPY_RUNS = ((0, 0, SRC_B_G),
             (B_Z, SRC_B_G + 2 * B_HEADS, 4 * BRANCH_WIDTH),
             (C_Z, SRC_C_LR + LANES, BRANCH_WIDTH + 3 * 2048))
REGROUP_ROWS = 64
REGROUP_STEP = 512


def _regroup_kernel(w_ref, o_ref):
    def shifted(src, width):
        base = src // LANES * LANES
        end = min(base + width + LANES, N_IN) if src != base else src + width
        return w_ref[:, base:end][:, src - base:src - base + width]

    for dst, src, width in COPY_RUNS:
        for c in range(0, width, REGROUP_STEP):
            o_ref[:, dst + c:dst + c + REGROUP_STEP] = shifted(src + c, REGROUP_STEP).astype(BF16)
    lane = lax.broadcasted_iota(jnp.int32, (REGROUP_ROWS, LANES), 1)
    gates = w_ref[:, SRC_B_G:SRC_B_G + LANES]
    o_ref[:, B_G:B_G + LANES] = jnp.where(lane < 2 * B_HEADS, gates, 0.0).astype(BF16)
    o_ref[:, C_LR:C_LR + LANES] = shifted(SRC_C_LR, LANES).astype(BF16)
    o_ref[:, C_LR + LANES:] = jnp.zeros((REGROUP_ROWS, N_PROJ - C_LR - LANES), BF16)


def _regroup_w_in(w_in):
    depth, d, n_in = w_in.shape
    assert n_in == N_IN and d % REGROUP_ROWS == 0
    return pl.pallas_call(
        _regroup_kernel,
        grid=(depth, d // REGROUP_ROWS),
        in_specs=[pl.BlockSpec((None, REGROUP_ROWS, n_in), lambda l, i: (l, i, 0))],
        out_specs=pl.BlockSpec((None, REGROUP_ROWS, N_PROJ), lambda l, i: (l, i, 0)),
        out_shape=jax.ShapeDtypeStruct((depth, d, N_PROJ), BF16),
        compiler_params=_params(("arbitrary", "arbitrary")),
        name="regroup_w_in",
    )(w_in)


def _rwkv_up_weights(w_up, a_up):
    top = jnp.concatenate([w_up, jnp.zeros_like(w_up)], axis=-1)
    bot = jnp.concatenate([jnp.zeros_like(a_up), a_up], axis=-1)
    w2 = jnp.concatenate([top, bot], axis=1)
    hi = w2.astype(BF16)
    lo = (w2 - hi.astype(F32)).astype(BF16)
    return hi, lo


def kernel(x, meta_tokens, norm_g, w_in, hgrn_lb_logits, hgrn_norm_g, mlstm_conv, mlstm_ig_b, mlstm_fg_b,
           mlstm_norm_g, rwkv_mu, rwkv_w0, rwkv_w_up, rwkv_a0, rwkv_a_up, rwkv_k_k, rwkv_k_a, rwkv_r_k,
           rwkv_ln_g, rwkv_ln_b, w_br, w_out, final_norm_g):
    bsz, seq, d = x.shape
    depth = w_in.shape[0]
    t_pad = PAD_FRONT + N_META + seq
    m = bsz * t_pad
    assert seq % (2 * CHUNK) == 0 and t_pad % CHUNK == 0

    tb = _divisor_tile(t_pad, 320, CHUNK)
    tm_in = _divisor_tile(m, 1664, LANES)
    tm_row = _divisor_tile(m, 640, LANES)
    tm_post = _divisor_tile(t_pad, 640, LANES)
    assert tm_post >= PAD_FRONT

    w_in_p = _regroup_w_in(w_in)
    w_br_b = w_br.astype(BF16)
    w_out_b = w_out.astype(BF16)
    p = jax.nn.softmax(hgrn_lb_logits.astype(F32), axis=0)
    lower_bounds = jnp.cumsum(p, axis=0) - p[0]
    gate_b = jnp.concatenate([mlstm_ig_b, mlstm_fg_b, jnp.zeros((depth, LANES - 2 * B_HEADS), F32)], axis=-1)
    w2h, w2l = _rwkv_up_weights(rwkv_w_up, rwkv_a_up)
    w = BRANCH_WIDTH
    mu_r, mu_k, mu_v, mu_lr = rwkv_mu[:, :w], rwkv_mu[:, w:2 * w], rwkv_mu[:, 2 * w:3 * w], rwkv_mu[:, 3 * w:]
    tri = jnp.asarray(np.tril(np.ones((CHUNK, CHUNK), np.float32)), BF16)
    levels = jnp.asarray(_hgrn2_levels())
    lane_head = np.arange(LANES) // C_DH
    ones_bd = jnp.asarray((lane_head[:, None] == lane_head[None, :]).astype(np.float32), BF16)

    meta = jnp.broadcast_to(meta_tokens[None].astype(F32), (bsz, N_META, d))
    h = jnp.concatenate([jnp.zeros((bsz, PAD_FRONT, d), F32), meta, x.astype(F32)], axis=1).reshape(m, d)

    xn = _rmsnorm(h, norm_g[0][None, :], BF16, tm_row)
    for l in range(depth):
        row = lambda a: a[l][None, :]
        proj = _inproj(xn, w_in_p, l, tm_in, 512)
        a_par = (row(lower_bounds), row(hgrn_norm_g), tri, levels)
        b_par = (mlstm_conv[l], row(gate_b), row(mlstm_norm_g), tri)
        c_par = (row(mu_r), row(mu_k), row(mu_v), row(mu_lr), row(rwkv_w0), row(rwkv_a0), row(rwkv_k_k),
                 row(rwkv_k_a), row(rwkv_r_k), row(rwkv_ln_g), row(rwkv_ln_b), w2h[l], w2l[l], tri, ones_bd)
        ya, yb, yc = _mixers(proj, a_par, b_par, c_par, bsz, t_pad, tb)
        g_next = norm_g[min(l + 1, depth - 1)][None, :]
        h, xn = _post(ya, yb, yc, proj, w_br_b, w_out_b, h, g_next, l, bsz, t_pad, tm_post, 512)

    return _final_norm(h, final_norm_g[None, :], bsz, t_pad, seq)
```

```python
import functools

import numpy as np
import jax
import jax.numpy as jnp
from jax import lax
from jax.experimental import pallas as pl
from jax.experimental.pallas import tpu as pltpu

F32 = jnp.float32
BF16 = jnp.bfloat16

N_META = 16
CHUNK = 64
NORM_EPS = 1e-6
NEG_BIG = -1e30
F_FLOOR = 1e-12
BRANCH_WIDTH = 1024
A_HEADS, A_DK = 8, 128
B_HEADS, B_DQK, B_DV, B_CONV = 4, 128, 256, 4
C_HEADS, C_DH, C_RANK = 16, 64, 64
C_LN_EPS = 64e-5
C_DECAY_MAX = float(np.exp(-0.5))

LANES = 128
SUBLANES = 8
PAD_FRONT = 2 * CHUNK - N_META
VMEM_LIMIT = 56 * 1024 * 1024

A_Q, A_F, A_I, A_Z = 0, 1024, 2048, 3072
B_Q, B_K, B_V, B_O, B_Z = 4096, 4608, 5120, 6144, 7168
C_R, C_K, C_V, C_Z = 8192, 9216, 10240, 11264
G_A, G_B, G_C = 12288, 14336, 16384
B_G = 18432
C_LR = 18560
N_PROJ = 18944


def _divisor_tile(n, target, mult):
    best = None
    for d in range(mult, min(n, target) + 1, mult):
        if n % d == 0:
            best = d
    if best is None:
        raise ValueError(f"no tile for {n} (multiple of {mult}, <= {target})")
    return best


def _params(sem):
    return pltpu.CompilerParams(dimension_semantics=sem, vmem_limit_bytes=VMEM_LIMIT)


def _sigmoid(x):
    return 0.5 * jnp.tanh(0.5 * x) + 0.5


def _silu(x):
    return x * _sigmoid(x)


def _dot(a, b):
    return jnp.dot(a, b, preferred_element_type=F32)


def _dot_nt(a, b):
    return lax.dot_general(a, b, (((1,), (1,)), ((), ())), preferred_element_type=F32)


def _split(x, pieces):
    out = []
    for _ in range(pieces - 1):
        part = x.astype(BF16)
        out.append(part)
        x = x - part.astype(F32)
    return out + [x.astype(BF16)]


def _cumsum_rows(tri, x, pieces=3):
    w = x.shape[1]
    r = _dot(tri, jnp.concatenate(_split(x, pieces), axis=1))
    acc = r[:, (pieces - 1) * w:]
    for i in range(pieces - 2, -1, -1):
        acc = acc + r[:, i * w:(i + 1) * w]
    return acc


def _row_bcast(x, r, n):
    return jnp.broadcast_to(x[r:r + 1, :], (n, x.shape[1]))


def _shift_rows(x, tail, j):
    rolled = pltpu.roll(x, j, 0)
    tail_r = pltpu.roll(tail, j, 0)
    row = lax.broadcasted_iota(jnp.int32, tail.shape, 0)
    first = jnp.where(row < j, tail_r, rolled[:SUBLANES])
    return jnp.concatenate([first, rolled[SUBLANES:]], axis=0)


def _tiles(x, w=LANES):
    return [x[:, i * w:(i + 1) * w] for i in range(x.shape[1] // w)]


def _rmsnorm_kernel(x_ref, g_ref, o_ref):
    x = x_ref[...]
    ms = jnp.mean(x * x, axis=-1, keepdims=True)
    o_ref[...] = (x * lax.rsqrt(ms + NORM_EPS) * g_ref[...]).astype(o_ref.dtype)


def _rmsnorm(h, g, out_dtype, tm):
    m, d = h.shape
    return pl.pallas_call(
        _rmsnorm_kernel,
        grid=(m // tm,),
        in_specs=[pl.BlockSpec((tm, d), lambda i: (i, 0)), pl.BlockSpec((1, d), lambda i: (0, 0))],
        out_specs=pl.BlockSpec((tm, d), lambda i: (i, 0)),
        out_shape=jax.ShapeDtypeStruct((m, d), out_dtype),
        compiler_params=_params(("arbitrary",)),
        name="rmsnorm",
    )(h, g)


def _final_norm_kernel(x_ref, g_ref, o_ref):
    x = x_ref[...]
    ms = jnp.mean(x * x, axis=-1, keepdims=True)
    o_ref[0] = x * lax.rsqrt(ms + NORM_EPS) * g_ref[...]


def _final_norm(h, g, bsz, t_pad, seq):
    d = h.shape[1]
    tm = CHUNK * 2
    per_b = t_pad // tm
    return pl.pallas_call(
        _final_norm_kernel,
        grid=(bsz, seq // tm),
        in_specs=[pl.BlockSpec((tm, d), lambda b, i: (b * per_b + i + 1, 0)),
                  pl.BlockSpec((1, d), lambda b, i: (0, 0))],
        out_specs=pl.BlockSpec((1, tm, d), lambda b, i: (b, i, 0)),
        out_shape=jax.ShapeDtypeStruct((bsz, seq, d), F32),
        compiler_params=_params(("arbitrary", "arbitrary")),
        name="final_norm",
    )(h, g)


def _inproj_kernel(x_ref, w_ref, o_ref):
    o_ref[...] = _dot(x_ref[...], w_ref[...])


def _inproj(xn, w, layer, tm, tn):
    m, d = xn.shape
    n = w.shape[2]
    return pl.pallas_call(
        _inproj_kernel,
        grid=(m // tm, n // tn),
        in_specs=[pl.BlockSpec((tm, d), lambda i, j: (i, 0)),
                  pl.BlockSpec((None, d, tn), lambda i, j: (layer, 0, j))],
        out_specs=pl.BlockSpec((tm, tn), lambda i, j: (i, j)),
        out_shape=jax.ShapeDtypeStruct((m, n), F32),
        compiler_params=_params(("arbitrary", "arbitrary")),
        name="inproj",
    )(xn, w)


def _hgrn2_levels():
    t = np.arange(CHUNK)[:, None]
    s = np.arange(CHUNK)[None, :]
    lv = np.zeros((CHUNK, CHUNK), np.int32)
    lv[(t // 4 == s // 4) & (s <= t)] = 1
    for lid, sz in ((2, 4), (3, 8), (4, 16), (5, 32)):
        lv[(t // (2 * sz) == s // (2 * sz)) & (t % (2 * sz) >= sz) & (s % (2 * sz) < sz)] = lid
    return lv


def _hgrn2_chunk(rows, q_ref, f_ref, i_ref, z_ref, lb_ref, g_ref, tri_ref, lv_ref, o_ref, st_ref):
    heads = range(A_HEADS)
    lv = lv_ref[...]
    sub = lax.broadcasted_iota(jnp.int32, (SUBLANES, BRANCH_WIDTH), 0)
    lb = lb_ref[...]
    fr = f_ref[rows, :]
    v = i_ref[rows, :]
    qs = _silu(q_ref[rows, :]) * (A_DK ** -0.5)
    sg = _sigmoid(fr)
    k = (1.0 - lb) * (1.0 - sg)
    lf = jnp.log(jnp.maximum(lb + (1.0 - lb) * sg, F_FLOOR))
    cg = _cumsum_rows(tri_ref[...], lf)
    yield
    cl = cg[CHUNK - 1:CHUNK, :]
    st = [st_ref[j] for j in heads]
    qg = _tiles((qs * jnp.exp(cg)).astype(BF16))
    o = [_dot_nt(qg[j], st[j].astype(BF16)) for j in heads]
    yield

    ref = jnp.concatenate(
        [jnp.where(sub < 4, _row_bcast(cg, 8 * g, SUBLANES), _row_bcast(cg, 8 * g + 4, SUBLANES))
         for g in range(CHUNK // SUBLANES)], axis=0)
    qt = _tiles((qs * jnp.exp(cg - ref)).astype(BF16))
    kt = _tiles((k * jnp.exp(ref - cg)).astype(BF16))
    att = [jnp.where(lv == 1, _dot_nt(qt[j], kt[j]), 0.0) for j in heads]
    yield
    row_i = lax.broadcasted_iota(jnp.int32, (CHUNK, BRANCH_WIDTH), 0)
    for lid, sz in ((2, 4), (3, 8), (4, 16), (5, 32)):
        ref = jnp.concatenate(
            [_row_bcast(cg, g * 2 * sz + sz - 1, 2 * sz) for g in range(CHUNK // (2 * sz))], axis=0)
        later = (row_i & sz) != 0
        x = _tiles((jnp.where(later, qs, k) * jnp.exp(-jnp.abs(cg - ref))).astype(BF16))
        att = [jnp.where(lv == lid, _dot_nt(x[j], x[j]), att[j]) for j in heads]
        yield

    vb = _tiles(v.astype(BF16))
    o = [o[j] + _dot(att[j].astype(BF16), vb[j]) for j in heads]
    yield
    kd = _tiles((k * jnp.exp(cl - cg)).astype(BF16))
    vt = [t.T.astype(BF16) for t in _tiles(v)]
    decay = _tiles(jnp.exp(cl))
    for j in heads:
        st_ref[j] = st[j] * decay[j] + _dot(vt[j], kd[j])
    yield
    on = [o[j] * lax.rsqrt(jnp.mean(o[j] * o[j], axis=-1, keepdims=True) + NORM_EPS) for j in heads]
    y = jnp.concatenate(on, axis=1) * g_ref[...] * _silu(z_ref[rows, :])
    o_ref[rows, :] = y.astype(o_ref.dtype)


def _mlstm_chunk(rows, q_ref, k_ref, v_ref, og_ref, z_ref, gt_ref, cwq_ref, cwk_ref, gb_ref, ng_ref, tri_ref,
                 o_ref, c_ref, m_ref, tq_ref, tk_ref):
    heads = range(B_HEADS)

    def conv(x, tail, w):
        acc = x * w[B_CONV - 1:B_CONV, :]
        for j in range(1, B_CONV):
            acc = acc + _shift_rows(x, tail, j) * w[B_CONV - 1 - j:B_CONV - j, :]
        return _silu(acc)

    t_i = lax.broadcasted_iota(jnp.int32, (CHUNK, CHUNK), 0)
    s_i = lax.broadcasted_iota(jnp.int32, (CHUNK, CHUNK), 1)
    causal = s_i <= t_i
    lane = lax.broadcasted_iota(jnp.int32, (CHUNK, LANES), 1)
    sub = lax.broadcasted_iota(jnp.int32, (LANES, CHUNK), 0)

    gates = gt_ref[rows, :] + gb_ref[...]
    lf = jnp.minimum(gates, 0.0) - jnp.log(1.0 + jnp.exp(-jnp.abs(gates)))
    bcum = _cumsum_rows(tri_ref[...], lf)
    yield
    qraw = q_ref[rows, :]
    kraw = k_ref[rows, :]
    qc = _tiles((conv(qraw, tq_ref[...], cwq_ref[...]) * (B_DQK ** -0.5)).astype(BF16), B_DQK)
    kc32 = _tiles(conv(kraw, tk_ref[...], cwk_ref[...]), B_DQK)
    kc = [t.astype(BF16) for t in kc32]
    tq_ref[...] = qraw[CHUNK - SUBLANES:, :]
    tk_ref[...] = kraw[CHUNK - SUBLANES:, :]
    c_aug = [c_ref[j] for j in heads]
    qk = [_dot_nt(qc[j], kc[j]) for j in heads]
    qcs = [_dot(qc[j], c_aug[j].astype(BF16)) for j in heads]
    yield

    gates_t = gates.T
    bcum_t = bcum.T
    ig_col = [jnp.sum(jnp.where(lane == j, gates, 0.0), axis=1, keepdims=True) for j in heads]
    b_col = [jnp.sum(jnp.where(lane == B_HEADS + j, bcum, 0.0), axis=1, keepdims=True) for j in heads]
    ig_row = [jnp.sum(jnp.where(sub == j, gates_t, 0.0), axis=0, keepdims=True) for j in heads]
    b_row = [jnp.sum(jnp.where(sub == B_HEADS + j, bcum_t, 0.0), axis=0, keepdims=True) for j in heads]
    yield

    ones = jnp.ones((CHUNK, LANES), BF16)
    v_aug = [jnp.concatenate([t.astype(BF16), ones], axis=1) for t in _tiles(v_ref[rows, :], B_DV)]
    m_prev = [m_ref[j][0:1, 0:1] for j in heads]
    log_w = [jnp.where(causal, b_col[j] - b_row[j] + ig_row[j], NEG_BIG) for j in heads]
    log_inter = [b_col[j] + m_prev[j] for j in heads]
    m_t = [jnp.maximum(log_inter[j], jnp.max(log_w[j], axis=-1, keepdims=True)) for j in heads]
    scores = [(qk[j] * jnp.exp(log_w[j] - m_t[j])).astype(BF16) for j in heads]
    numden = [_dot(scores[j], v_aug[j]) + jnp.exp(log_inter[j] - m_t[j]) * qcs[j] for j in heads]
    yield

    b_end = [b_col[j][CHUNK - 1:CHUNK, :] for j in heads]
    log_s = [b_end[j] - b_col[j] + ig_col[j] for j in heads]
    m_new = [jnp.maximum(b_end[j] + m_prev[j], jnp.max(log_s[j], axis=0, keepdims=True)) for j in heads]
    kw = [(kc32[j] * jnp.exp(log_s[j] - m_new[j])).T.astype(BF16) for j in heads]
    for j in heads:
        c_ref[j] = jnp.exp(b_end[j] + m_prev[j] - m_new[j]) * c_aug[j] + _dot(kw[j], v_aug[j])
        m_ref[j] = jnp.broadcast_to(m_new[j], (SUBLANES, LANES))
    yield

    hid = [numden[j][:, :B_DV] / jnp.maximum(jnp.abs(numden[j][:, B_DV:B_DV + 1]), jnp.exp(-m_t[j]))
           for j in heads]
    hc = [hid[j] - jnp.mean(hid[j], axis=-1, keepdims=True) for j in heads]
    hn = [hc[j] * lax.rsqrt(jnp.mean(hc[j] * hc[j], axis=-1, keepdims=True) + NORM_EPS) for j in heads]
    y = jnp.concatenate(hn, axis=1) * ng_ref[...] * _sigmoid(og_ref[rows, :]) * _silu(z_ref[rows, :])
    o_ref[rows, :] = y.astype(o_ref.dtype)


def _rwkv_chunk(rows, r_ref, k_ref, v_ref, z_ref, lr_ref, mur_ref, muk_ref, muv_ref, mulr_ref, w0_ref, a0_ref,
                kk_ref, ka_ref, rk_ref, lng_ref, lnb_ref, w2h_ref, w2l_ref, tri_ref, ones_ref,
                o_ref, st_ref, tr_ref, tk_ref, tv_ref, tlr_ref):
    npairs = BRANCH_WIDTH // LANES
    pairs = range(npairs)

    def lerp(x_ref, tail_ref, mu_ref):
        x = x_ref[rows, :]
        out = x + (_shift_rows(x, tail_ref[...], 1) - x) * mu_ref[...]
        tail_ref[...] = x[CHUNK - SUBLANES:, :]
        return out

    tri = tri_ref[...]
    ones_bd = ones_ref[...]
    lane = lax.broadcasted_iota(jnp.int32, (CHUNK, LANES), 1)
    t_i = lax.broadcasted_iota(jnp.int32, (CHUNK, LANES), 0)
    s_i = lane % C_DH
    head0 = lane < C_DH
    strict = s_i < t_i
    incl = s_i <= t_i
    bd_r = lax.broadcasted_iota(jnp.int32, (LANES, LANES), 0) // C_DH
    bd_c = lax.broadcasted_iota(jnp.int32, (LANES, LANES), 1) // C_DH
    blockdiag = bd_r == bd_c

    def stack(x):
        return jnp.concatenate([jnp.where(head0, x, 0.0), jnp.where(head0, 0.0, x)], axis=0)

    def gsum(x):
        res = _dot(jnp.concatenate(_tiles(x.astype(BF16)), axis=0), ones_bd)
        return jnp.concatenate([res[p * CHUNK:(p + 1) * CHUNK] for p in pairs], axis=1)

    lr = lerp(lr_ref, tlr_ref, mulr_ref)
    lr = jnp.where(head0, jnp.tanh(lr), lr)
    lr_hi = lr.astype(BF16)
    lr_lo = (lr - lr_hi.astype(F32)).astype(BF16)
    up2 = _dot(jnp.concatenate([lr_hi, lr_lo], axis=0), w2h_ref[...])
    up_lo = _dot(lr_hi, w2l_ref[...])
    k0 = lerp(k_ref, tk_ref, muk_ref)
    kk = k0 * kk_ref[...]
    kk_ss = gsum(kk * kk)
    yield
    up = up2[:CHUNK] + up2[CHUNK:] + up_lo
    lw = -C_DECAY_MAX * _sigmoid(w0_ref[...] + up[:, :BRANCH_WIDTH])
    cw = _cumsum_rows(tri, lw, pieces=2)
    yield
    r = lerp(r_ref, tr_ref, mur_ref)
    v = lerp(v_ref, tv_ref, muv_ref)
    a = _sigmoid(a0_ref[...] + up[:, BRANCH_WIDTH:])
    kk = kk / jnp.maximum(jnp.sqrt(kk_ss), 1e-12)
    k = k0 * (1.0 + (a - 1.0) * ka_ref[...])
    b = kk * a
    bonus_s = gsum(r * k * rk_ref[...])
    e_c = jnp.exp(cw)
    e_n = jnp.exp(-cw)
    at = _tiles(-kk * jnp.exp(cw - lw))
    rt = _tiles(r * e_c)
    bt = _tiles(b * e_n)
    kt = _tiles(k * e_n)
    vp = _tiles(v)
    decay = _tiles(e_c[CHUNK - 1:CHUNK, :])
    st = [st_ref[p] for p in pairs]

    ar = [jnp.concatenate([at[p], rt[p]], axis=0).astype(BF16) for p in pairs]
    bk = [jnp.concatenate([stack(bt[p]), stack(kt[p])], axis=0).astype(BF16) for p in pairs]
    pm = [_dot_nt(ar[p], bk[p]) for p in pairs]
    a_s = [_dot_nt(ar[p], st[p].astype(BF16)) for p in pairs]
    yield
    v_st = [stack(vp[p]).astype(BF16) for p in pairs]
    l_ak = [jnp.where(strict, pm[p][:CHUNK, LANES:], 0.0).astype(BF16) for p in pairs]
    u = [a_s[p][:CHUNK] + _dot(l_ak[p], v_st[p]) for p in pairs]
    yield
    mk = [jnp.where(strict, pm[p][:CHUNK, :LANES], 0.0) for p in pairs]
    steps = CHUNK.bit_length() - 1
    for it in range(steps - 1):
        res = [_dot(mk[p].astype(BF16),
                    jnp.concatenate([stack(u[p]), stack(mk[p])], axis=1).astype(BF16)) for p in pairs]
        yield
        u = [u[p] + res[p][:, :LANES] for p in pairs]
        mk = [res[p][:, LANES:] for p in pairs]
    u = [u[p] + _dot(mk[p].astype(BF16), stack(u[p]).astype(BF16)) for p in pairs]
    yield

    rbk = [jnp.concatenate([jnp.where(incl, pm[p][CHUNK:, :LANES], 0.0),
                            jnp.where(incl, pm[p][CHUNK:, LANES:], 0.0)], axis=1).astype(BF16) for p in pairs]
    y = [a_s[p][CHUNK:] + _dot(rbk[p], jnp.concatenate([stack(u[p]).astype(BF16), v_st[p]], axis=0))
         for p in pairs]
    uvt = [jnp.concatenate([u[p], vp[p]], axis=0).T.astype(BF16) for p in pairs]
    bkt = [jnp.concatenate([bt[p], kt[p]], axis=0).astype(BF16) for p in pairs]
    for p in pairs:
        st_ref[p] = (st[p] + jnp.where(blockdiag, _dot(uvt[p], bkt[p]), 0.0)) * decay[p]
    yield

    y = jnp.concatenate(y, axis=1)
    mean = gsum(y) * (1.0 / C_DH)
    yield
    yc = y - mean
    var = gsum(yc * yc) * (1.0 / C_DH)
    yield
    yn = yc * lax.rsqrt(var + C_LN_EPS) * lng_ref[...] + lnb_ref[...]
    o_ref[rows, :] = ((yn + bonus_s * v) * _silu(z_ref[rows, :])).astype(o_ref.dtype)


N_A_IN, N_B_IN, N_C_IN = 8, 11, 20
N_A_SCR, N_B_SCR, N_C_SCR = 1, 4, 5


def _mixers_kernel(*refs, chunks):
    refs = list(refs)

    def take(n):
        out = refs[:n]
        del refs[:n]
        return out

    a_in, b_in, c_in = take(N_A_IN), take(N_B_IN), take(N_C_IN)
    ya_ref, yb_ref, yc_ref = take(3)
    a_scr, b_scr, c_scr = take(N_A_SCR), take(N_B_SCR), take(N_C_SCR)

    @pl.when(pl.program_id(1) == 0)
    def _():
        for ref in a_scr + b_scr + c_scr:
            ref[...] = jnp.zeros_like(ref)

    def chunk(c, carry):
        rows = pl.ds(pl.multiple_of(c * CHUNK, CHUNK), CHUNK)
        live = [_rwkv_chunk(rows, *c_in, yc_ref, *c_scr),
                _hgrn2_chunk(rows, *a_in, ya_ref, *a_scr),
                _mlstm_chunk(rows, *b_in, yb_ref, *b_scr)]
        while live:
            for gen in list(live):
                if next(gen, "done") == "done":
                    live.remove(gen)
        return carry

    lax.fori_loop(0, chunks, chunk, 0)


def _mixers(proj, a_par, b_par, c_par, bsz, t_pad, tb):
    m = proj.shape[0]
    nt = t_pad // tb
    w = BRANCH_WIDTH
    qw = B_HEADS * B_DQK

    def col(off, width):
        return pl.BlockSpec((tb, width), lambda b, t: (b * nt + t, off // width))

    def whole(a):
        return pl.BlockSpec(a.shape, lambda b, t: (0,) * a.ndim)

    a_cols = [col(A_Q, w), col(A_F, w), col(A_I, w), col(A_Z, w)]
    b_cols = [col(B_Q, qw), col(B_K, qw), col(B_V, w), col(B_O, w), col(B_Z, w), col(B_G, LANES)]
    c_cols = [col(C_R, w), col(C_K, w), col(C_V, w), col(C_Z, w), col(C_LR, LANES)]
    conv_w = b_par[0]
    b_specs = [pl.BlockSpec((B_CONV, qw), lambda b, t: (0, 0)),
               pl.BlockSpec((B_CONV, qw), lambda b, t: (0, 1))] + [whole(a) for a in b_par[1:]]
    operands = ([proj] * len(a_cols) + list(a_par) + [proj] * len(b_cols) + [conv_w] + list(b_par)
                + [proj] * len(c_cols) + list(c_par))
    in_specs = (a_cols + [whole(a) for a in a_par] + b_cols + b_specs + c_cols + [whole(a) for a in c_par])
    assert len(a_cols) + len(a_par) == N_A_IN and len(b_cols) + len(b_specs) == N_B_IN
    assert len(c_cols) + len(c_par) == N_C_IN
    out_spec = pl.BlockSpec((tb, w), lambda b, t: (b * nt + t, 0))
    return pl.pallas_call(
        functools.partial(_mixers_kernel, chunks=tb // CHUNK),
        grid=(bsz, nt),
        in_specs=in_specs,
        out_specs=[out_spec] * 3,
        out_shape=[jax.ShapeDtypeStruct((m, w), BF16)] * 3,
        scratch_shapes=[pltpu.VMEM((A_HEADS, LANES, LANES), F32),
                        pltpu.VMEM((B_HEADS, B_DQK, B_DV + LANES), F32),
                        pltpu.VMEM((B_HEADS, SUBLANES, LANES), F32),
                        pltpu.VMEM((SUBLANES, qw), F32),
                        pltpu.VMEM((SUBLANES, qw), F32),
                        pltpu.VMEM((w // LANES, LANES, LANES), F32),
                        pltpu.VMEM((SUBLANES, w), F32),
                        pltpu.VMEM((SUBLANES, w), F32),
                        pltpu.VMEM((SUBLANES, w), F32),
                        pltpu.VMEM((SUBLANES, LANES), F32)],
        compiler_params=_params(("arbitrary", "arbitrary")),
        name="mixers",
    )(*operands)


def _post_kernel(ya_ref, yb_ref, yc_ref, ga_ref, gb_ref, gc_ref, wbr_ref, wout_ref, h_ref, gn_ref,
                 o_ref, xn_ref, mg_ref, hrow_ref, *, nsub, tn):
    j = pl.program_id(2)

    @pl.when(j < nsub)
    def _():
        acc = _sigmoid(ga_ref[...]) * _dot(ya_ref[...], wbr_ref[0])
        acc = acc + _sigmoid(gb_ref[...]) * _dot(yb_ref[...], wbr_ref[1])
        acc = acc + _sigmoid(gc_ref[...]) * _dot(yc_ref[...], wbr_ref[2])
        mg_ref[j] = acc.astype(BF16)

    @pl.when(j >= nsub)
    def _():
        new = h_ref[...]
        for kk in range(nsub):
            new = new + _dot(mg_ref[kk], wout_ref[kk * tn:(kk + 1) * tn, :])
        row = lax.broadcasted_iota(jnp.int32, new.shape, 0)
        keep = jnp.logical_or(pl.program_id(1) > 0, row >= PAD_FRONT)
        new = jnp.where(keep, new, 0.0)
        o_ref[...] = new
        hrow_ref[j - nsub] = new

    @pl.when(j == 2 * nsub - 1)
    def _():
        ss = jnp.sum(hrow_ref[0] * hrow_ref[0], axis=-1, keepdims=True)
        for kk in range(1, nsub):
            ss = ss + jnp.sum(hrow_ref[kk] * hrow_ref[kk], axis=-1, keepdims=True)
        scale = lax.rsqrt(ss * (1.0 / (nsub * tn)) + NORM_EPS)
        for kk in range(nsub):
            cols = slice(kk * tn, (kk + 1) * tn)
            xn_ref[:, cols] = (hrow_ref[kk] * scale * gn_ref[:, cols]).astype(xn_ref.dtype)


def _post(ya, yb, yc, proj, w_br, w_out, h, g_next, layer, bsz, t_pad, tm, tn):
    m, d = h.shape
    nt = t_pad // tm
    nsub = d // tn
    y_spec = pl.BlockSpec((tm, BRANCH_WIDTH), lambda b, t, j: (b * nt + t, 0))

    def gate(off):
        return pl.BlockSpec((tm, tn), lambda b, t, j: (b * nt + t, off // tn + jnp.minimum(j, nsub - 1)))

    def out_col(j):
        return jnp.maximum(j - nsub, 0)

    return pl.pallas_call(
        functools.partial(_post_kernel, nsub=nsub, tn=tn),
        grid=(bsz, nt, 2 * nsub),
        in_specs=[y_spec, y_spec, y_spec, gate(G_A), gate(G_B), gate(G_C),
                  pl.BlockSpec((None, 3, BRANCH_WIDTH, tn), lambda b, t, j: (layer, 0, 0, jnp.minimum(j, nsub - 1))),
                  pl.BlockSpec((None, d, tn), lambda b, t, j: (layer, 0, out_col(j))),
                  pl.BlockSpec((tm, tn), lambda b, t, j: (b * nt + t, out_col(j))),
                  pl.BlockSpec((1, d), lambda b, t, j: (0, 0))],
        out_specs=[pl.BlockSpec((tm, tn), lambda b, t, j: (b * nt + t, out_col(j))),
                   pl.BlockSpec((tm, d), lambda b, t, j: (b * nt + t, 0))],
        out_shape=[jax.ShapeDtypeStruct((m, d), F32), jax.ShapeDtypeStruct((m, d), BF16)],
        scratch_shapes=[pltpu.VMEM((nsub, tm, tn), BF16), pltpu.VMEM((nsub, tm, tn), F32)],
        input_output_aliases={8: 0},
        compiler_params=_params(("arbitrary", "arbitrary", "arbitrary")),
        name="post",
    )(ya, yb, yc, proj, proj, proj, w_br, w_out, h, g_next)


SRC_B_G = B_O + BRANCH_WIDTH
SRC_C_LR = SRC_B_G + 2 * B_HEADS + 4 * BRANCH_WIDTH
N_IN = SRC_C_LR + LANES + BRANCH_WIDTH + 3 * 2048
RG_TILE = 512
RG_SRC_TILES = -(-N_IN // RG_TILE)
RG_ROWS = 256
RG_RUNS = ((0, SRC_B_G // RG_TILE, 0),
           (B_Z // RG_TILE, C_Z // RG_TILE, SRC_B_G + 2 * B_HEADS - B_Z),
           (C_Z // RG_TILE, B_G // RG_TILE, SRC_C_LR + LANES - C_Z))


def _regroup_kernel(x_ref, o_ref, t2_ref, g_ref, c_ref):
    s = pl.program_id(1)
    d = t2_ref.shape[0]

    @pl.when(s >= 1)
    def _():
        t2_ref[:, :RG_TILE] = t2_ref[:, RG_TILE:]

    @pl.when(s < RG_SRC_TILES)
    def _():
        for r in range(RG_TILE // LANES):
            t2_ref[:, RG_TILE + r * LANES:RG_TILE + (r + 1) * LANES] = x_ref[r * LANES:(r + 1) * LANES, :].T

    @pl.when(s == SRC_B_G // RG_TILE)
    def _():
        g_ref[...] = t2_ref[:, RG_TILE:RG_TILE + LANES]

    @pl.when(s == SRC_C_LR // RG_TILE)
    def _():
        c_ref[...] = t2_ref[:, RG_TILE:RG_TILE + 2 * LANES]

    def emit(shift):
        base = shift // LANES * LANES
        for r in range(0, d, RG_ROWS):
            if shift == base:
                tile = t2_ref[r:r + RG_ROWS, base:base + RG_TILE]
            else:
                tile = t2_ref[r:r + RG_ROWS, base:base + RG_TILE + LANES][:, shift - base:shift - base + RG_TILE]
            o_ref[r:r + RG_ROWS, :] = tile.astype(BF16)

    t = s - 1
    for first, end, shift in RG_RUNS:
        pl.when(jnp.logical_and(t >= first, t < end))(functools.partial(emit, shift))

    @pl.when(t == B_G // RG_TILE)
    def _():
        lane = lax.broadcasted_iota(jnp.int32, (RG_ROWS, LANES), 1)
        off = SRC_C_LR % RG_TILE
        for r in range(0, d, RG_ROWS):
            rows = slice(r, r + RG_ROWS)
            o_ref[rows, :LANES] = jnp.where(lane < 2 * B_HEADS, g_ref[rows, :], 0.0).astype(BF16)
            o_ref[rows, LANES:2 * LANES] = c_ref[rows, :][:, off:off + LANES].astype(BF16)
            o_ref[rows, 2 * LANES:] = jnp.zeros((RG_ROWS, RG_TILE - 2 * LANES), BF16)


def _regroup_w_in(w_in):
    depth, d, n_in = w_in.shape
    assert n_in == N_IN and d % RG_ROWS == 0 and N_PROJ == (B_G // RG_TILE + 1) * RG_TILE
    assert C_LR == B_G + LANES and SRC_C_LR % RG_TILE + LANES <= 2 * LANES
    return pl.pallas_call(
        _regroup_kernel,
        grid=(depth, RG_SRC_TILES + 1),
        in_specs=[pl.BlockSpec((None, RG_TILE, d), lambda l, s: (l, jnp.minimum(s, RG_SRC_TILES - 1), 0))],
        out_specs=pl.BlockSpec((None, d, RG_TILE), lambda l, s: (l, 0, jnp.maximum(s - 1, 0))),
        out_shape=jax.ShapeDtypeStruct((depth, d, N_PROJ), BF16),
        scratch_shapes=[pltpu.VMEM((d, 2 * RG_TILE), F32), pltpu.VMEM((d, LANES), F32),
                        pltpu.VMEM((d, 2 * LANES), F32)],
        compiler_params=_params(("arbitrary", "arbitrary")),
        name="regroup_w_in",
    )(jnp.swapaxes(w_in, 1, 2))


def _rwkv_up_weights(w_up, a_up):
    top = jnp.concatenate([w_up, jnp.zeros_like(w_up)], axis=-1)
    bot = jnp.concatenate([jnp.zeros_like(a_up), a_up], axis=-1)
    w2 = jnp.concatenate([top, bot], axis=1)
    hi = w2.astype(BF16)
    lo = (w2 - hi.astype(F32)).astype(BF16)
    return hi, lo


def kernel(x, meta_tokens, norm_g, w_in, hgrn_lb_logits, hgrn_norm_g, mlstm_conv, mlstm_ig_b, mlstm_fg_b,
           mlstm_norm_g, rwkv_mu, rwkv_w0, rwkv_w_up, rwkv_a0, rwkv_a_up, rwkv_k_k, rwkv_k_a, rwkv_r_k,
           rwkv_ln_g, rwkv_ln_b, w_br, w_out, final_norm_g):
    bsz, seq, d = x.shape
    depth = w_in.shape[0]
    t_pad = PAD_FRONT + N_META + seq
    m = bsz * t_pad
    assert seq % (2 * CHUNK) == 0 and t_pad % CHUNK == 0

    tb = _divisor_tile(t_pad, 320, CHUNK)
    tm_in = _divisor_tile(m, 1664, LANES)
    tm_row = _divisor_tile(m, 640, LANES)
    tm_post = _divisor_tile(t_pad, 640, LANES)
    assert tm_post >= PAD_FRONT

    w_in_p = _regroup_w_in(w_in)
    w_br_b = w_br.astype(BF16)
    w_out_b = w_out.astype(BF16)
    p = jax.nn.softmax(hgrn_lb_logits.astype(F32), axis=0)
    lower_bounds = jnp.cumsum(p, axis=0) - p[0]
    gate_b = jnp.concatenate([mlstm_ig_b, mlstm_fg_b, jnp.zeros((depth, LANES - 2 * B_HEADS), F32)], axis=-1)
    w2h, w2l = _rwkv_up_weights(rwkv_w_up, rwkv_a_up)
    w = BRANCH_WIDTH
    mu_r, mu_k, mu_v, mu_lr = rwkv_mu[:, :w], rwkv_mu[:, w:2 * w], rwkv_mu[:, 2 * w:3 * w], rwkv_mu[:, 3 * w:]
    tri = jnp.asarray(np.tril(np.ones((CHUNK, CHUNK), np.float32)), BF16)
    levels = jnp.asarray(_hgrn2_levels())
    lane_head = np.arange(LANES) // C_DH
    ones_bd = jnp.asarray((lane_head[:, None] == lane_head[None, :]).astype(np.float32), BF16)

    meta = jnp.broadcast_to(meta_tokens[None].astype(F32), (bsz, N_META, d))
    h = jnp.concatenate([jnp.zeros((bsz, PAD_FRONT, d), F32), meta, x.astype(F32)], axis=1).reshape(m, d)

    xn = _rmsnorm(h, norm_g[0][None, :], BF16, tm_row)
    for l in range(depth):
        row = lambda a: a[l][None, :]
        proj = _inproj(xn, w_in_p, l, tm_in, 512)
        a_par = (row(lower_bounds), row(hgrn_norm_g), tri, levels)
        b_par = (mlstm_conv[l], row(gate_b), row(mlstm_norm_g), tri)
        c_par = (row(mu_r), row(mu_k), row(mu_v), row(mu_lr), row(rwkv_w0), row(rwkv_a0), row(rwkv_k_k),
                 row(rwkv_k_a), row(rwkv_r_k), row(rwkv_ln_g), row(rwkv_ln_b), w2h[l], w2l[l], tri, ones_bd)
        ya, yb, yc = _mixers(proj, a_par, b_par, c_par, bsz, t_pad, tb)
        g_next = norm_g[min(l + 1, depth - 1)][None, :]
        h, xn = _post(ya, yb, yc, proj, w_br_b, w_out_b, h, g_next, l, bsz, t_pad, tm_post, 512)

    return _final_norm(h, final_norm_g[None, :], bsz, t_pad, seq)
```

```python
import functools

import numpy as np
import jax
import jax.numpy as jnp
from jax import lax
from jax.experimental import pallas as pl
from jax.experimental.pallas import tpu as pltpu

F32 = jnp.float32
BF16 = jnp.bfloat16

N_META = 16
CHUNK = 64
NORM_EPS = 1e-6
NEG_BIG = -1e30
F_FLOOR = 1e-12
BRANCH_WIDTH = 1024
A_HEADS, A_DK = 8, 128
B_HEADS, B_DQK, B_DV, B_CONV = 4, 128, 256, 4
C_HEADS, C_DH, C_RANK = 16, 64, 64
C_LN_EPS = 64e-5
C_DECAY_MAX = float(np.exp(-0.5))

LANES = 128
SUBLANES = 8
PAD_FRONT = 2 * CHUNK - N_META
VMEM_LIMIT = 56 * 1024 * 1024

A_Q, A_F, A_I, A_Z = 0, 1024, 2048, 3072
B_Q, B_K, B_V, B_O, B_Z = 4096, 4608, 5120, 6144, 7168
C_R, C_K, C_V, C_Z = 8192, 9216, 10240, 11264
G_A, G_B, G_C = 12288, 14336, 16384
B_G = 18432
C_LR = 18560
N_PROJ = 18944


def _divisor_tile(n, target, mult):
    best = None
    for d in range(mult, min(n, target) + 1, mult):
        if n % d == 0:
            best = d
    if best is None:
        raise ValueError(f"no tile for {n} (multiple of {mult}, <= {target})")
    return best


def _params(sem):
    return pltpu.CompilerParams(dimension_semantics=sem, vmem_limit_bytes=VMEM_LIMIT)


def _sigmoid(x):
    return 0.5 * jnp.tanh(0.5 * x) + 0.5


def _silu(x):
    return x * _sigmoid(x)


def _dot(a, b):
    return jnp.dot(a, b, preferred_element_type=F32)


def _dot_nt(a, b):
    return lax.dot_general(a, b, (((1,), (1,)), ((), ())), preferred_element_type=F32)


def _split(x, pieces):
    out = []
    for _ in range(pieces - 1):
        part = x.astype(BF16)
        out.append(part)
        x = x - part.astype(F32)
    return out + [x.astype(BF16)]


def _cumsum_rows(tri, x, pieces=3):
    w = x.shape[1]
    r = _dot(tri, jnp.concatenate(_split(x, pieces), axis=1))
    acc = r[:, (pieces - 1) * w:]
    for i in range(pieces - 2, -1, -1):
        acc = acc + r[:, i * w:(i + 1) * w]
    return acc


def _row_bcast(x, r, n):
    return jnp.broadcast_to(x[r:r + 1, :], (n, x.shape[1]))


def _shift_rows(x, tail, j):
    rolled = pltpu.roll(x, j, 0)
    tail_r = pltpu.roll(tail, j, 0)
    row = lax.broadcasted_iota(jnp.int32, tail.shape, 0)
    first = jnp.where(row < j, tail_r, rolled[:SUBLANES])
    return jnp.concatenate([first, rolled[SUBLANES:]], axis=0)


def _tiles(x, w=LANES):
    return [x[:, i * w:(i + 1) * w] for i in range(x.shape[1] // w)]


def _rmsnorm_kernel(x_ref, g_ref, o_ref):
    x = x_ref[...]
    ms = jnp.mean(x * x, axis=-1, keepdims=True)
    o_ref[...] = (x * lax.rsqrt(ms + NORM_EPS) * g_ref[...]).astype(o_ref.dtype)


def _rmsnorm(h, g, out_dtype, tm):
    m, d = h.shape
    return pl.pallas_call(
        _rmsnorm_kernel,
        grid=(m // tm,),
        in_specs=[pl.BlockSpec((tm, d), lambda i: (i, 0)), pl.BlockSpec((1, d), lambda i: (0, 0))],
        out_specs=pl.BlockSpec((tm, d), lambda i: (i, 0)),
        out_shape=jax.ShapeDtypeStruct((m, d), out_dtype),
        compiler_params=_params(("arbitrary",)),
        name="rmsnorm",
    )(h, g)


def _final_norm_kernel(x_ref, g_ref, o_ref):
    x = x_ref[...]
    ms = jnp.mean(x * x, axis=-1, keepdims=True)
    o_ref[0] = x * lax.rsqrt(ms + NORM_EPS) * g_ref[...]


def _final_norm(h, g, bsz, t_pad, seq):
    d = h.shape[1]
    tm = CHUNK * 2
    per_b = t_pad // tm
    return pl.pallas_call(
        _final_norm_kernel,
        grid=(bsz, seq // tm),
        in_specs=[pl.BlockSpec((tm, d), lambda b, i: (b * per_b + i + 1, 0)),
                  pl.BlockSpec((1, d), lambda b, i: (0, 0))],
        out_specs=pl.BlockSpec((1, tm, d), lambda b, i: (b, i, 0)),
        out_shape=jax.ShapeDtypeStruct((bsz, seq, d), F32),
        compiler_params=_params(("arbitrary", "arbitrary")),
        name="final_norm",
    )(h, g)


def _inproj_kernel(x_ref, rs_ref, w_ref, o_ref):
    o_ref[...] = _dot(x_ref[...], w_ref[...]) * rs_ref[:, :1]


def _inproj(hg, rs, w, layer, tm, tn):
    m, d = hg.shape
    n = w.shape[2]
    return pl.pallas_call(
        _inproj_kernel,
        grid=(m // tm, n // tn),
        in_specs=[pl.BlockSpec((tm, d), lambda i, j: (i, 0)),
                  pl.BlockSpec((tm, LANES), lambda i, j: (i, 0)),
                  pl.BlockSpec((None, d, tn), lambda i, j: (layer, 0, j))],
        out_specs=pl.BlockSpec((tm, tn), lambda i, j: (i, j)),
        out_shape=jax.ShapeDtypeStruct((m, n), F32),
        compiler_params=_params(("arbitrary", "arbitrary")),
        name="inproj",
    )(hg, rs, w)


def _hgrn2_levels():
    t = np.arange(CHUNK)[:, None]
    s = np.arange(CHUNK)[None, :]
    lv = np.zeros((CHUNK, CHUNK), np.int32)
    lv[(t // 4 == s // 4) & (s <= t)] = 1
    for lid, sz in ((2, 4), (3, 8), (4, 16), (5, 32)):
        lv[(t // (2 * sz) == s // (2 * sz)) & (t % (2 * sz) >= sz) & (s % (2 * sz) < sz)] = lid
    return lv


def _hgrn2_chunk(rows, q_ref, f_ref, i_ref, z_ref, lb_ref, g_ref, tri_ref, lv_ref, o_ref, st_ref):
    heads = range(A_HEADS)
    lv = lv_ref[...]
    sub = lax.broadcasted_iota(jnp.int32, (SUBLANES, BRANCH_WIDTH), 0)
    lb = lb_ref[...]
    fr = f_ref[rows, :]
    v = i_ref[rows, :]
    qs = _silu(q_ref[rows, :]) * (A_DK ** -0.5)
    sg = _sigmoid(fr)
    k = (1.0 - lb) * (1.0 - sg)
    lf = jnp.log(jnp.maximum(lb + (1.0 - lb) * sg, F_FLOOR))
    cg = _cumsum_rows(tri_ref[...], lf)
    yield
    cl = cg[CHUNK - 1:CHUNK, :]
    st = [st_ref[j] for j in heads]
    qg = _tiles((qs * jnp.exp(cg)).astype(BF16))
    o = [_dot_nt(qg[j], st[j].astype(BF16)) for j in heads]
    yield

    ref = jnp.concatenate(
        [jnp.where(sub < 4, _row_bcast(cg, 8 * g, SUBLANES), _row_bcast(cg, 8 * g + 4, SUBLANES))
         for g in range(CHUNK // SUBLANES)], axis=0)
    qt = _tiles((qs * jnp.exp(cg - ref)).astype(BF16))
    kt = _tiles((k * jnp.exp(ref - cg)).astype(BF16))
    att = [jnp.where(lv == 1, _dot_nt(qt[j], kt[j]), 0.0) for j in heads]
    yield
    row_i = lax.broadcasted_iota(jnp.int32, (CHUNK, BRANCH_WIDTH), 0)
    for lid, sz in ((2, 4), (3, 8), (4, 16), (5, 32)):
        ref = jnp.concatenate(
            [_row_bcast(cg, g * 2 * sz + sz - 1, 2 * sz) for g in range(CHUNK // (2 * sz))], axis=0)
        later = (row_i & sz) != 0
        x = _tiles((jnp.where(later, qs, k) * jnp.exp(-jnp.abs(cg - ref))).astype(BF16))
        att = [jnp.where(lv == lid, _dot_nt(x[j], x[j]), att[j]) for j in heads]
        yield

    vb = _tiles(v.astype(BF16))
    o = [o[j] + _dot(att[j].astype(BF16), vb[j]) for j in heads]
    yield
    kd = _tiles((k * jnp.exp(cl - cg)).astype(BF16))
    vt = [t.T.astype(BF16) for t in _tiles(v)]
    decay = _tiles(jnp.exp(cl))
    for j in heads:
        st_ref[j] = st[j] * decay[j] + _dot(vt[j], kd[j])
    yield
    on = [o[j] * lax.rsqrt(jnp.mean(o[j] * o[j], axis=-1, keepdims=True) + NORM_EPS) for j in heads]
    y = jnp.concatenate(on, axis=1) * g_ref[...] * _silu(z_ref[rows, :])
    o_ref[rows, :] = y.astype(o_ref.dtype)


def _mlstm_chunk(rows, q_ref, k_ref, v_ref, og_ref, z_ref, gt_ref, cwq_ref, cwk_ref, gb_ref, ng_ref, tri_ref,
                 o_ref, c_ref, m_ref, tq_ref, tk_ref):
    heads = range(B_HEADS)

    def conv(x, tail, w):
        acc = x * w[B_CONV - 1:B_CONV, :]
        for j in range(1, B_CONV):
            acc = acc + _shift_rows(x, tail, j) * w[B_CONV - 1 - j:B_CONV - j, :]
        return _silu(acc)

    t_i = lax.broadcasted_iota(jnp.int32, (CHUNK, CHUNK), 0)
    s_i = lax.broadcasted_iota(jnp.int32, (CHUNK, CHUNK), 1)
    causal = s_i <= t_i
    lane = lax.broadcasted_iota(jnp.int32, (CHUNK, LANES), 1)
    sub = lax.broadcasted_iota(jnp.int32, (LANES, CHUNK), 0)

    gates = gt_ref[rows, :] + gb_ref[...]
    lf = jnp.minimum(gates, 0.0) - jnp.log(1.0 + jnp.exp(-jnp.abs(gates)))
    bcum = _cumsum_rows(tri_ref[...], lf)
    yield
    qraw = q_ref[rows, :]
    kraw = k_ref[rows, :]
    qc = _tiles((conv(qraw, tq_ref[...], cwq_ref[...]) * (B_DQK ** -0.5)).astype(BF16), B_DQK)
    kc32 = _tiles(conv(kraw, tk_ref[...], cwk_ref[...]), B_DQK)
    kc = [t.astype(BF16) for t in kc32]
    tq_ref[...] = qraw[CHUNK - SUBLANES:, :]
    tk_ref[...] = kraw[CHUNK - SUBLANES:, :]
    c_aug = [c_ref[j] for j in heads]
    qk = [_dot_nt(qc[j], kc[j]) for j in heads]
    qcs = [_dot(qc[j], c_aug[j].astype(BF16)) for j in heads]
    yield

    gates_t = gates.T
    bcum_t = bcum.T
    ig_col = [jnp.sum(jnp.where(lane == j, gates, 0.0), axis=1, keepdims=True) for j in heads]
    b_col = [jnp.sum(jnp.where(lane == B_HEADS + j, bcum, 0.0), axis=1, keepdims=True) for j in heads]
    ig_row = [jnp.sum(jnp.where(sub == j, gates_t, 0.0), axis=0, keepdims=True) for j in heads]
    b_row = [jnp.sum(jnp.where(sub == B_HEADS + j, bcum_t, 0.0), axis=0, keepdims=True) for j in heads]
    yield

    ones = jnp.ones((CHUNK, LANES), BF16)
    v_aug = [jnp.concatenate([t.astype(BF16), ones], axis=1) for t in _tiles(v_ref[rows, :], B_DV)]
    m_prev = [m_ref[j][0:1, 0:1] for j in heads]
    log_w = [jnp.where(causal, b_col[j] - b_row[j] + ig_row[j], NEG_BIG) for j in heads]
    log_inter = [b_col[j] + m_prev[j] for j in heads]
    m_t = [jnp.maximum(log_inter[j], jnp.max(log_w[j], axis=-1, keepdims=True)) for j in heads]
    scores = [(qk[j] * jnp.exp(log_w[j] - m_t[j])).astype(BF16) for j in heads]
    numden = [_dot(scores[j], v_aug[j]) + jnp.exp(log_inter[j] - m_t[j]) * qcs[j] for j in heads]
    yield

    b_end = [b_col[j][CHUNK - 1:CHUNK, :] for j in heads]
    log_s = [b_end[j] - b_col[j] + ig_col[j] for j in heads]
    m_new = [jnp.maximum(b_end[j] + m_prev[j], jnp.max(log_s[j], axis=0, keepdims=True)) for j in heads]
    kw = [(kc32[j] * jnp.exp(log_s[j] - m_new[j])).T.astype(BF16) for j in heads]
    for j in heads:
        c_ref[j] = jnp.exp(b_end[j] + m_prev[j] - m_new[j]) * c_aug[j] + _dot(kw[j], v_aug[j])
        m_ref[j] = jnp.broadcast_to(m_new[j], (SUBLANES, LANES))
    yield

    hid = [numden[j][:, :B_DV] / jnp.maximum(jnp.abs(numden[j][:, B_DV:B_DV + 1]), jnp.exp(-m_t[j]))
           for j in heads]
    hc = [hid[j] - jnp.mean(hid[j], axis=-1, keepdims=True) for j in heads]
    hn = [hc[j] * lax.rsqrt(jnp.mean(hc[j] * hc[j], axis=-1, keepdims=True) + NORM_EPS) for j in heads]
    y = jnp.concatenate(hn, axis=1) * ng_ref[...] * _sigmoid(og_ref[rows, :]) * _silu(z_ref[rows, :])
    o_ref[rows, :] = y.astype(o_ref.dtype)


def _rwkv_chunk(rows, r_ref, k_ref, v_ref, z_ref, lr_ref, mur_ref, muk_ref, muv_ref, mulr_ref, w0_ref, a0_ref,
                kk_ref, ka_ref, rk_ref, lng_ref, lnb_ref, w2h_ref, w2l_ref, tri_ref, ones_ref,
                o_ref, st_ref, tr_ref, tk_ref, tv_ref, tlr_ref):
    npairs = BRANCH_WIDTH // LANES
    pairs = range(npairs)

    def lerp(x_ref, tail_ref, mu_ref):
        x = x_ref[rows, :]
        out = x + (_shift_rows(x, tail_ref[...], 1) - x) * mu_ref[...]
        tail_ref[...] = x[CHUNK - SUBLANES:, :]
        return out

    tri = tri_ref[...]
    ones_bd = ones_ref[...]
    lane = lax.broadcasted_iota(jnp.int32, (CHUNK, LANES), 1)
    t_i = lax.broadcasted_iota(jnp.int32, (CHUNK, LANES), 0)
    s_i = lane % C_DH
    head0 = lane < C_DH
    strict = s_i < t_i
    incl = s_i <= t_i
    bd_r = lax.broadcasted_iota(jnp.int32, (LANES, LANES), 0) // C_DH
    bd_c = lax.broadcasted_iota(jnp.int32, (LANES, LANES), 1) // C_DH
    blockdiag = bd_r == bd_c

    def stack(x):
        return jnp.concatenate([jnp.where(head0, x, 0.0), jnp.where(head0, 0.0, x)], axis=0)

    def gsum(x):
        res = _dot(jnp.concatenate(_tiles(x.astype(BF16)), axis=0), ones_bd)
        return jnp.concatenate([res[p * CHUNK:(p + 1) * CHUNK] for p in pairs], axis=1)

    lr = lerp(lr_ref, tlr_ref, mulr_ref)
    lr = jnp.where(head0, jnp.tanh(lr), lr)
    lr_hi = lr.astype(BF16)
    lr_lo = (lr - lr_hi.astype(F32)).astype(BF16)
    up2 = _dot(jnp.concatenate([lr_hi, lr_lo], axis=0), w2h_ref[...])
    up_lo = _dot(lr_hi, w2l_ref[...])
    k0 = lerp(k_ref, tk_ref, muk_ref)
    kk = k0 * kk_ref[...]
    kk_ss = gsum(kk * kk)
    yield
    up = up2[:CHUNK] + up2[CHUNK:] + up_lo
    lw = -C_DECAY_MAX * _sigmoid(w0_ref[...] + up[:, :BRANCH_WIDTH])
    cw = _cumsum_rows(tri, lw, pieces=2)
    yield
    r = lerp(r_ref, tr_ref, mur_ref)
    v = lerp(v_ref, tv_ref, muv_ref)
    a = _sigmoid(a0_ref[...] + up[:, BRANCH_WIDTH:])
    kk = kk / jnp.maximum(jnp.sqrt(kk_ss), 1e-12)
    k = k0 * (1.0 + (a - 1.0) * ka_ref[...])
    b = kk * a
    bonus_s = gsum(r * k * rk_ref[...])
    e_c = jnp.exp(cw)
    e_n = jnp.exp(-cw)
    at = _tiles(-kk * jnp.exp(cw - lw))
    rt = _tiles(r * e_c)
    bt = _tiles(b * e_n)
    kt = _tiles(k * e_n)
    vp = _tiles(v)
    decay = _tiles(e_c[CHUNK - 1:CHUNK, :])
    st = [st_ref[p] for p in pairs]

    ar = [jnp.concatenate([at[p], rt[p]], axis=0).astype(BF16) for p in pairs]
    bk = [jnp.concatenate([stack(bt[p]), stack(kt[p])], axis=0).astype(BF16) for p in pairs]
    pm = [_dot_nt(ar[p], bk[p]) for p in pairs]
    a_s = [_dot_nt(ar[p], st[p].astype(BF16)) for p in pairs]
    yield
    v_st = [stack(vp[p]).astype(BF16) for p in pairs]
    l_ak = [jnp.where(strict, pm[p][:CHUNK, LANES:], 0.0).astype(BF16) for p in pairs]
    u = [a_s[p][:CHUNK] + _dot(l_ak[p], v_st[p]) for p in pairs]
    yield
    mk = [jnp.where(strict, pm[p][:CHUNK, :LANES], 0.0) for p in pairs]
    steps = CHUNK.bit_length() - 1
    for it in range(steps - 1):
        res = [_dot(mk[p].astype(BF16),
                    jnp.concatenate([stack(u[p]), stack(mk[p])], axis=1).astype(BF16)) for p in pairs]
        yield
        u = [u[p] + res[p][:, :LANES] for p in pairs]
        mk = [res[p][:, LANES:] for p in pairs]
    u = [u[p] + _dot(mk[p].astype(BF16), stack(u[p]).astype(BF16)) for p in pairs]
    yield

    rbk = [jnp.concatenate([jnp.where(incl, pm[p][CHUNK:, :LANES], 0.0),
                            jnp.where(incl, pm[p][CHUNK:, LANES:], 0.0)], axis=1).astype(BF16) for p in pairs]
    y = [a_s[p][CHUNK:] + _dot(rbk[p], jnp.concatenate([stack(u[p]).astype(BF16), v_st[p]], axis=0))
         for p in pairs]
    uvt = [jnp.concatenate([u[p], vp[p]], axis=0).T.astype(BF16) for p in pairs]
    bkt = [jnp.concatenate([bt[p], kt[p]], axis=0).astype(BF16) for p in pairs]
    for p in pairs:
        st_ref[p] = (st[p] + jnp.where(blockdiag, _dot(uvt[p], bkt[p]), 0.0)) * decay[p]
    yield

    y = jnp.concatenate(y, axis=1)
    mean = gsum(y) * (1.0 / C_DH)
    yield
    yc = y - mean
    var = gsum(yc * yc) * (1.0 / C_DH)
    yield
    yn = yc * lax.rsqrt(var + C_LN_EPS) * lng_ref[...] + lnb_ref[...]
    o_ref[rows, :] = ((yn + bonus_s * v) * _silu(z_ref[rows, :])).astype(o_ref.dtype)


N_A_IN, N_B_IN, N_C_IN = 8, 11, 20
N_A_SCR, N_B_SCR, N_C_SCR = 1, 4, 5


def _mixers_kernel(*refs, chunks):
    refs = list(refs)

    def take(n):
        out = refs[:n]
        del refs[:n]
        return out

    a_in, b_in, c_in = take(N_A_IN), take(N_B_IN), take(N_C_IN)
    ya_ref, yb_ref, yc_ref = take(3)
    a_scr, b_scr, c_scr = take(N_A_SCR), take(N_B_SCR), take(N_C_SCR)

    @pl.when(pl.program_id(1) == 0)
    def _():
        for ref in a_scr + b_scr + c_scr:
            ref[...] = jnp.zeros_like(ref)

    def chunk(c, carry):
        rows = pl.ds(pl.multiple_of(c * CHUNK, CHUNK), CHUNK)
        live = [_rwkv_chunk(rows, *c_in, yc_ref, *c_scr),
                _hgrn2_chunk(rows, *a_in, ya_ref, *a_scr),
                _mlstm_chunk(rows, *b_in, yb_ref, *b_scr)]
        while live:
            for gen in list(live):
                if next(gen, "done") == "done":
                    live.remove(gen)
        return carry

    lax.fori_loop(0, chunks, chunk, 0)


def _mixers(proj, a_par, b_par, c_par, bsz, t_pad, tb):
    m = proj.shape[0]
    nt = t_pad // tb
    w = BRANCH_WIDTH
    qw = B_HEADS * B_DQK

    def col(off, width):
        return pl.BlockSpec((tb, width), lambda b, t: (b * nt + t, off // width))

    def whole(a):
        return pl.BlockSpec(a.shape, lambda b, t: (0,) * a.ndim)

    a_cols = [col(A_Q, w), col(A_F, w), col(A_I, w), col(A_Z, w)]
    b_cols = [col(B_Q, qw), col(B_K, qw), col(B_V, w), col(B_O, w), col(B_Z, w), col(B_G, LANES)]
    c_cols = [col(C_R, w), col(C_K, w), col(C_V, w), col(C_Z, w), col(C_LR, LANES)]
    conv_w = b_par[0]
    b_specs = [pl.BlockSpec((B_CONV, qw), lambda b, t: (0, 0)),
               pl.BlockSpec((B_CONV, qw), lambda b, t: (0, 1))] + [whole(a) for a in b_par[1:]]
    operands = ([proj] * len(a_cols) + list(a_par) + [proj] * len(b_cols) + [conv_w] + list(b_par)
                + [proj] * len(c_cols) + list(c_par))
    in_specs = (a_cols + [whole(a) for a in a_par] + b_cols + b_specs + c_cols + [whole(a) for a in c_par])
    assert len(a_cols) + len(a_par) == N_A_IN and len(b_cols) + len(b_specs) == N_B_IN
    assert len(c_cols) + len(c_par) == N_C_IN
    out_spec = pl.BlockSpec((tb, w), lambda b, t: (b * nt + t, 0))
    return pl.pallas_call(
        functools.partial(_mixers_kernel, chunks=tb // CHUNK),
        grid=(bsz, nt),
        in_specs=in_specs,
        out_specs=[out_spec] * 3,
        out_shape=[jax.ShapeDtypeStruct((m, w), BF16)] * 3,
        scratch_shapes=[pltpu.VMEM((A_HEADS, LANES, LANES), F32),
                        pltpu.VMEM((B_HEADS, B_DQK, B_DV + LANES), F32),
                        pltpu.VMEM((B_HEADS, SUBLANES, LANES), F32),
                        pltpu.VMEM((SUBLANES, qw), F32),
                        pltpu.VMEM((SUBLANES, qw), F32),
                        pltpu.VMEM((w // LANES, LANES, LANES), F32),
                        pltpu.VMEM((SUBLANES, w), F32),
                        pltpu.VMEM((SUBLANES, w), F32),
                        pltpu.VMEM((SUBLANES, w), F32),
                        pltpu.VMEM((SUBLANES, LANES), F32)],
        compiler_params=_params(("arbitrary", "arbitrary")),
        name="mixers",
    )(*operands)


def _post_kernel(ya_ref, yb_ref, yc_ref, ga_ref, gb_ref, gc_ref, wbr_ref, wout_ref, h_ref, gn_ref,
                 o_ref, hg_ref, rs_ref, mg_ref, ss_ref, *, nsub, tn):
    j = pl.program_id(2)

    @pl.when(j == 0)
    def _():
        ss_ref[...] = jnp.zeros_like(ss_ref)

    @pl.when(j < nsub)
    def _():
        acc = _sigmoid(ga_ref[...]) * _dot(ya_ref[...], wbr_ref[0])
        acc = acc + _sigmoid(gb_ref[...]) * _dot(yb_ref[...], wbr_ref[1])
        acc = acc + _sigmoid(gc_ref[...]) * _dot(yc_ref[...], wbr_ref[2])
        mg_ref[j] = acc.astype(BF16)

    @pl.when(j >= nsub)
    def _():
        new = h_ref[...]
        for kk in range(nsub):
            new = new + _dot(mg_ref[kk], wout_ref[kk * tn:(kk + 1) * tn, :])
        row = lax.broadcasted_iota(jnp.int32, new.shape, 0)
        keep = jnp.logical_or(pl.program_id(1) > 0, row >= PAD_FRONT)
        new = jnp.where(keep, new, 0.0)
        o_ref[...] = new
        hg_ref[...] = (new * gn_ref[...]).astype(hg_ref.dtype)
        ss_ref[...] += jnp.sum(new * new, axis=-1, keepdims=True)

    @pl.when(j == 2 * nsub - 1)
    def _():
        rs_ref[...] = lax.rsqrt(ss_ref[...] * (1.0 / (nsub * tn)) + NORM_EPS)


def _post(ya, yb, yc, proj, w_br, w_out, h, g_next, layer, bsz, t_pad, tm, tn):
    m, d = h.shape
    nt = t_pad // tm
    nsub = d // tn
    y_spec = pl.BlockSpec((tm, BRANCH_WIDTH), lambda b, t, j: (b * nt + t, 0))

    def gate(off):
        return pl.BlockSpec((tm, tn), lambda b, t, j: (b * nt + t, off // tn + jnp.minimum(j, nsub - 1)))

    def out_col(j):
        return jnp.maximum(j - nsub, 0)

    return pl.pallas_call(
        functools.partial(_post_kernel, nsub=nsub, tn=tn),
        grid=(bsz, nt, 2 * nsub),
        in_specs=[y_spec, y_spec, y_spec, gate(G_A), gate(G_B), gate(G_C),
                  pl.BlockSpec((None, 3, BRANCH_WIDTH, tn), lambda b, t, j: (layer, 0, 0, jnp.minimum(j, nsub - 1))),
                  pl.BlockSpec((None, d, tn), lambda b, t, j: (layer, 0, out_col(j))),
                  pl.BlockSpec((tm, tn), lambda b, t, j: (b * nt + t, out_col(j))),
                  pl.BlockSpec((1, tn), lambda b, t, j: (0, out_col(j)))],
        out_specs=[pl.BlockSpec((tm, tn), lambda b, t, j: (b * nt + t, out_col(j))),
                   pl.BlockSpec((tm, tn), lambda b, t, j: (b * nt + t, out_col(j))),
                   pl.BlockSpec((tm, LANES), lambda b, t, j: (b * nt + t, 0))],
        out_shape=[jax.ShapeDtypeStruct((m, d), F32), jax.ShapeDtypeStruct((m, d), BF16),
                   jax.ShapeDtypeStruct((m, LANES), F32)],
        scratch_shapes=[pltpu.VMEM((nsub, tm, tn), BF16), pltpu.VMEM((tm, LANES), F32)],
        input_output_aliases={8: 0},
        compiler_params=_params(("arbitrary", "arbitrary", "arbitrary")),
        name="post",
    )(ya, yb, yc, proj, proj, proj, w_br, w_out, h, g_next)


SRC_B_G = B_O + BRANCH_WIDTH
SRC_C_LR = SRC_B_G + 2 * B_HEADS + 4 * BRANCH_WIDTH
N_IN = SRC_C_LR + LANES + BRANCH_WIDTH + 3 * 2048
RG_TILE = 512
RG_SRC_TILES = -(-N_IN // RG_TILE)
RG_ROWS = 256
RG_RUNS = ((0, SRC_B_G // RG_TILE, 0),
           (B_Z // RG_TILE, C_Z // RG_TILE, SRC_B_G + 2 * B_HEADS - B_Z),
           (C_Z // RG_TILE, B_G // RG_TILE, SRC_C_LR + LANES - C_Z))


def _regroup_kernel(x_ref, o_ref, t2_ref, g_ref, c_ref):
    s = pl.program_id(1)
    d = t2_ref.shape[0]

    @pl.when(s >= 1)
    def _():
        t2_ref[:, :RG_TILE] = t2_ref[:, RG_TILE:]

    @pl.when(s < RG_SRC_TILES)
    def _():
        for r in range(RG_TILE // LANES):
            t2_ref[:, RG_TILE + r * LANES:RG_TILE + (r + 1) * LANES] = x_ref[r * LANES:(r + 1) * LANES, :].T

    @pl.when(s == SRC_B_G // RG_TILE)
    def _():
        g_ref[...] = t2_ref[:, RG_TILE:RG_TILE + LANES]

    @pl.when(s == SRC_C_LR // RG_TILE)
    def _():
        c_ref[...] = t2_ref[:, RG_TILE:RG_TILE + 2 * LANES]

    def emit(shift):
        base = shift // LANES * LANES
        for r in range(0, d, RG_ROWS):
            if shift == base:
                tile = t2_ref[r:r + RG_ROWS, base:base + RG_TILE]
            else:
                tile = t2_ref[r:r + RG_ROWS, base:base + RG_TILE + LANES][:, shift - base:shift - base + RG_TILE]
            o_ref[r:r + RG_ROWS, :] = tile.astype(BF16)

    t = s - 1
    for first, end, shift in RG_RUNS:
        pl.when(jnp.logical_and(t >= first, t < end))(functools.partial(emit, shift))

    @pl.when(t == B_G // RG_TILE)
    def _():
        lane = lax.broadcasted_iota(jnp.int32, (RG_ROWS, LANES), 1)
        off = SRC_C_LR % RG_TILE
        for r in range(0, d, RG_ROWS):
            rows = slice(r, r + RG_ROWS)
            o_ref[rows, :LANES] = jnp.where(lane < 2 * B_HEADS, g_ref[rows, :], 0.0).astype(BF16)
            o_ref[rows, LANES:2 * LANES] = c_ref[rows, :][:, off:off + LANES].astype(BF16)
            o_ref[rows, 2 * LANES:] = jnp.zeros((RG_ROWS, RG_TILE - 2 * LANES), BF16)


def _regroup_w_in(w_in):
    depth, d, n_in = w_in.shape
    assert n_in == N_IN and d % RG_ROWS == 0 and N_PROJ == (B_G // RG_TILE + 1) * RG_TILE
    assert C_LR == B_G + LANES and SRC_C_LR % RG_TILE + LANES <= 2 * LANES
    return pl.pallas_call(
        _regroup_kernel,
        grid=(depth, RG_SRC_TILES + 1),
        in_specs=[pl.BlockSpec((None, RG_TILE, d), lambda l, s: (l, jnp.minimum(s, RG_SRC_TILES - 1), 0))],
        out_specs=pl.BlockSpec((None, d, RG_TILE), lambda l, s: (l, 0, jnp.maximum(s - 1, 0))),
        out_shape=jax.ShapeDtypeStruct((depth, d, N_PROJ), BF16),
        scratch_shapes=[pltpu.VMEM((d, 2 * RG_TILE), F32), pltpu.VMEM((d, LANES), F32),
                        pltpu.VMEM((d, 2 * LANES), F32)],
        compiler_params=_params(("arbitrary", "arbitrary")),
        name="regroup_w_in",
    )(jnp.swapaxes(w_in, 1, 2))


def _rwkv_up_weights(w_up, a_up):
    top = jnp.concatenate([w_up, jnp.zeros_like(w_up)], axis=-1)
    bot = jnp.concatenate([jnp.zeros_like(a_up), a_up], axis=-1)
    w2 = jnp.concatenate([top, bot], axis=1)
    hi = w2.astype(BF16)
    lo = (w2 - hi.astype(F32)).astype(BF16)
    return hi, lo


def kernel(x, meta_tokens, norm_g, w_in, hgrn_lb_logits, hgrn_norm_g, mlstm_conv, mlstm_ig_b, mlstm_fg_b,
           mlstm_norm_g, rwkv_mu, rwkv_w0, rwkv_w_up, rwkv_a0, rwkv_a_up, rwkv_k_k, rwkv_k_a, rwkv_r_k,
           rwkv_ln_g, rwkv_ln_b, w_br, w_out, final_norm_g):
    bsz, seq, d = x.shape
    depth = w_in.shape[0]
    t_pad = PAD_FRONT + N_META + seq
    m = bsz * t_pad
    assert seq % (2 * CHUNK) == 0 and t_pad % CHUNK == 0

    tb = _divisor_tile(t_pad, 320, CHUNK)
    tm_in = _divisor_tile(m, 3328, LANES)
    tm_row = _divisor_tile(m, 640, LANES)
    tm_post = _divisor_tile(t_pad, 1040, 2 * SUBLANES)
    assert tm_post >= PAD_FRONT

    w_in_p = _regroup_w_in(w_in)
    w_br_b = w_br.astype(BF16)
    w_out_b = w_out.astype(BF16)
    p = jax.nn.softmax(hgrn_lb_logits.astype(F32), axis=0)
    lower_bounds = jnp.cumsum(p, axis=0) - p[0]
    gate_b = jnp.concatenate([mlstm_ig_b, mlstm_fg_b, jnp.zeros((depth, LANES - 2 * B_HEADS), F32)], axis=-1)
    w2h, w2l = _rwkv_up_weights(rwkv_w_up, rwkv_a_up)
    w = BRANCH_WIDTH
    mu_r, mu_k, mu_v, mu_lr = rwkv_mu[:, :w], rwkv_mu[:, w:2 * w], rwkv_mu[:, 2 * w:3 * w], rwkv_mu[:, 3 * w:]
    tri = jnp.asarray(np.tril(np.ones((CHUNK, CHUNK), np.float32)), BF16)
    levels = jnp.asarray(_hgrn2_levels())
    lane_head = np.arange(LANES) // C_DH
    ones_bd = jnp.asarray((lane_head[:, None] == lane_head[None, :]).astype(np.float32), BF16)

    meta = jnp.broadcast_to(meta_tokens[None].astype(F32), (bsz, N_META, d))
    h = jnp.concatenate([jnp.zeros((bsz, PAD_FRONT, d), F32), meta, x.astype(F32)], axis=1).reshape(m, d)

    hg = _rmsnorm(h, norm_g[0][None, :], BF16, tm_row)
    rs = jnp.ones((m, LANES), F32)
    for l in range(depth):
        row = lambda a: a[l][None, :]
        proj = _inproj(hg, rs, w_in_p, l, tm_in, 512)
        a_par = (row(lower_bounds), row(hgrn_norm_g), tri, levels)
        b_par = (mlstm_conv[l], row(gate_b), row(mlstm_norm_g), tri)
        c_par = (row(mu_r), row(mu_k), row(mu_v), row(mu_lr), row(rwkv_w0), row(rwkv_a0), row(rwkv_k_k),
                 row(rwkv_k_a), row(rwkv_r_k), row(rwkv_ln_g), row(rwkv_ln_b), w2h[l], w2l[l], tri, ones_bd)
        ya, yb, yc = _mixers(proj, a_par, b_par, c_par, bsz, t_pad, tb)
        g_next = norm_g[min(l + 1, depth - 1)][None, :]
        h, hg, rs = _post(ya, yb, yc, proj, w_br_b, w_out_b, h, g_next, l, bsz, t_pad, tm_post, 512)

    return _final_norm(h, final_norm_g[None, :], bsz, t_pad, seq)
```

```python
import functools

import numpy as np
import jax
import jax.numpy as jnp
from jax import lax
from jax.experimental import pallas as pl
from jax.experimental.pallas import tpu as pltpu

F32 = jnp.float32
BF16 = jnp.bfloat16

N_META = 16
CHUNK = 64
NORM_EPS = 1e-6
NEG_BIG = -1e30
F_FLOOR = 1e-12
BRANCH_WIDTH = 1024
A_HEADS, A_DK = 8, 128
B_HEADS, B_DQK, B_DV, B_CONV = 4, 128, 256, 4
C_HEADS, C_DH, C_RANK = 16, 64, 64
C_LN_EPS = 64e-5
C_DECAY_MAX = float(np.exp(-0.5))

LANES = 128
SUBLANES = 8
PAD_FRONT = 2 * CHUNK - N_META
VMEM_LIMIT = 56 * 1024 * 1024

A_Q, A_F, A_I, A_Z = 0, 1024, 2048, 3072
B_Q, B_K, B_V, B_O, B_Z = 4096, 4608, 5120, 6144, 7168
C_R, C_K, C_V, C_Z = 8192, 9216, 10240, 11264
G_BASE = 12288
N_BRANCH = 3
D_GATE = 2048
GATE_TILE = 512
B_G = 18432
C_LR = 18560
N_PROJ = 18944


def _divisor_tile(n, target, mult):
    best = None
    for d in range(mult, min(n, target) + 1, mult):
        if n % d == 0:
            best = d
    if best is None:
        raise ValueError(f"no tile for {n} (multiple of {mult}, <= {target})")
    return best


def _params(sem):
    return pltpu.CompilerParams(dimension_semantics=sem, vmem_limit_bytes=VMEM_LIMIT)


def _sigmoid(x):
    return 0.5 * jnp.tanh(0.5 * x) + 0.5


def _silu(x):
    return x * _sigmoid(x)


def _dot(a, b):
    return jnp.dot(a, b, preferred_element_type=F32)


def _dot_nt(a, b):
    return lax.dot_general(a, b, (((1,), (1,)), ((), ())), preferred_element_type=F32)


def _split(x, pieces):
    out = []
    for _ in range(pieces - 1):
        part = x.astype(BF16)
        out.append(part)
        x = x - part.astype(F32)
    return out + [x.astype(BF16)]


def _cumsum_rows(tri, x, pieces=3):
    w = x.shape[1]
    r = _dot(tri, jnp.concatenate(_split(x, pieces), axis=1))
    acc = r[:, (pieces - 1) * w:]
    for i in range(pieces - 2, -1, -1):
        acc = acc + r[:, i * w:(i + 1) * w]
    return acc


def _row_bcast(x, r, n):
    return jnp.broadcast_to(x[r:r + 1, :], (n, x.shape[1]))


def _shift_rows(x, tail, j):
    rolled = pltpu.roll(x, j, 0)
    tail_r = pltpu.roll(tail, j, 0)
    row = lax.broadcasted_iota(jnp.int32, tail.shape, 0)
    first = jnp.where(row < j, tail_r, rolled[:SUBLANES])
    return jnp.concatenate([first, rolled[SUBLANES:]], axis=0)


def _tiles(x, w=LANES):
    return [x[:, i * w:(i + 1) * w] for i in range(x.shape[1] // w)]


def _rmsnorm_kernel(x_ref, g_ref, o_ref):
    x = x_ref[...]
    ms = jnp.mean(x * x, axis=-1, keepdims=True)
    o_ref[...] = (x * lax.rsqrt(ms + NORM_EPS) * g_ref[...]).astype(o_ref.dtype)


def _rmsnorm(h, g, out_dtype, tm):
    m, d = h.shape
    return pl.pallas_call(
        _rmsnorm_kernel,
        grid=(m // tm,),
        in_specs=[pl.BlockSpec((tm, d), lambda i: (i, 0)), pl.BlockSpec((1, d), lambda i: (0, 0))],
        out_specs=pl.BlockSpec((tm, d), lambda i: (i, 0)),
        out_shape=jax.ShapeDtypeStruct((m, d), out_dtype),
        compiler_params=_params(("arbitrary",)),
        name="rmsnorm",
    )(h, g)


def _final_norm_kernel(x_ref, g_ref, o_ref):
    x = x_ref[...]
    ms = jnp.mean(x * x, axis=-1, keepdims=True)
    o_ref[0] = x * lax.rsqrt(ms + NORM_EPS) * g_ref[...]


def _final_norm(h, g, bsz, t_pad, seq):
    d = h.shape[1]
    tm = CHUNK * 2
    per_b = t_pad // tm
    return pl.pallas_call(
        _final_norm_kernel,
        grid=(bsz, seq // tm),
        in_specs=[pl.BlockSpec((tm, d), lambda b, i: (b * per_b + i + 1, 0)),
                  pl.BlockSpec((1, d), lambda b, i: (0, 0))],
        out_specs=pl.BlockSpec((1, tm, d), lambda b, i: (b, i, 0)),
        out_shape=jax.ShapeDtypeStruct((bsz, seq, d), F32),
        compiler_params=_params(("arbitrary", "arbitrary")),
        name="final_norm",
    )(h, g)


def _inproj_kernel(x_ref, rs_ref, w_ref, o_ref):
    o_ref[...] = _dot(x_ref[...], w_ref[...]) * rs_ref[:, :1]


def _inproj(hg, rs, w, layer, tm, tn):
    m, d = hg.shape
    n = w.shape[2]
    return pl.pallas_call(
        _inproj_kernel,
        grid=(m // tm, n // tn),
        in_specs=[pl.BlockSpec((tm, d), lambda i, j: (i, 0)),
                  pl.BlockSpec((tm, LANES), lambda i, j: (i, 0)),
                  pl.BlockSpec((None, d, tn), lambda i, j: (layer, 0, j))],
        out_specs=pl.BlockSpec((tm, tn), lambda i, j: (i, j)),
        out_shape=jax.ShapeDtypeStruct((m, n), F32),
        compiler_params=_params(("arbitrary", "arbitrary")),
        name="inproj",
    )(hg, rs, w)


def _hgrn2_levels():
    t = np.arange(CHUNK)[:, None]
    s = np.arange(CHUNK)[None, :]
    lv = np.zeros((CHUNK, CHUNK), np.int32)
    lv[(t // 4 == s // 4) & (s <= t)] = 1
    for lid, sz in ((2, 4), (3, 8), (4, 16), (5, 32)):
        lv[(t // (2 * sz) == s // (2 * sz)) & (t % (2 * sz) >= sz) & (s % (2 * sz) < sz)] = lid
    return lv


def _hgrn2_chunk(rows, cols, q_ref, f_ref, i_ref, z_ref, lb_ref, g_ref, tri_ref, lv_ref, o_ref, st_ref):
    heads = range(A_HEADS)
    lv = lv_ref[...]
    sub = lax.broadcasted_iota(jnp.int32, (SUBLANES, BRANCH_WIDTH), 0)
    lb = lb_ref[...]
    fr = f_ref[rows, :]
    v = i_ref[rows, :]
    qs = _silu(q_ref[rows, :]) * (A_DK ** -0.5)
    sg = _sigmoid(fr)
    k = (1.0 - lb) * (1.0 - sg)
    lf = jnp.log(jnp.maximum(lb + (1.0 - lb) * sg, F_FLOOR))
    cg = _cumsum_rows(tri_ref[...], lf)
    yield
    cl = cg[CHUNK - 1:CHUNK, :]
    st = [st_ref[j] for j in heads]
    qg = _tiles((qs * jnp.exp(cg)).astype(BF16))
    o = [_dot_nt(qg[j], st[j].astype(BF16)) for j in heads]
    yield

    ref = jnp.concatenate(
        [jnp.where(sub < 4, _row_bcast(cg, 8 * g, SUBLANES), _row_bcast(cg, 8 * g + 4, SUBLANES))
         for g in range(CHUNK // SUBLANES)], axis=0)
    qt = _tiles((qs * jnp.exp(cg - ref)).astype(BF16))
    kt = _tiles((k * jnp.exp(ref - cg)).astype(BF16))
    att = [jnp.where(lv == 1, _dot_nt(qt[j], kt[j]), 0.0) for j in heads]
    yield
    row_i = lax.broadcasted_iota(jnp.int32, (CHUNK, BRANCH_WIDTH), 0)
    for lid, sz in ((2, 4), (3, 8), (4, 16), (5, 32)):
        ref = jnp.concatenate(
            [_row_bcast(cg, g * 2 * sz + sz - 1, 2 * sz) for g in range(CHUNK // (2 * sz))], axis=0)
        later = (row_i & sz) != 0
        x = _tiles((jnp.where(later, qs, k) * jnp.exp(-jnp.abs(cg - ref))).astype(BF16))
        att = [jnp.where(lv == lid, _dot_nt(x[j], x[j]), att[j]) for j in heads]
        yield

    vb = _tiles(v.astype(BF16))
    o = [o[j] + _dot(att[j].astype(BF16), vb[j]) for j in heads]
    yield
    kd = _tiles((k * jnp.exp(cl - cg)).astype(BF16))
    vt = [t.T.astype(BF16) for t in _tiles(v)]
    decay = _tiles(jnp.exp(cl))
    for j in heads:
        st_ref[j] = st[j] * decay[j] + _dot(vt[j], kd[j])
    yield
    on = [o[j] * lax.rsqrt(jnp.mean(o[j] * o[j], axis=-1, keepdims=True) + NORM_EPS) for j in heads]
    y = jnp.concatenate(on, axis=1) * g_ref[...] * _silu(z_ref[rows, :])
    o_ref[rows, cols] = y.astype(o_ref.dtype)


def _mlstm_chunk(rows, cols, q_ref, k_ref, v_ref, og_ref, z_ref, gt_ref, cwq_ref, cwk_ref, gb_ref, ng_ref, tri_ref,
                 o_ref, c_ref, m_ref, tq_ref, tk_ref):
    heads = range(B_HEADS)

    def conv(x, tail, w):
        acc = x * w[B_CONV - 1:B_CONV, :]
        for j in range(1, B_CONV):
            acc = acc + _shift_rows(x, tail, j) * w[B_CONV - 1 - j:B_CONV - j, :]
        return _silu(acc)

    t_i = lax.broadcasted_iota(jnp.int32, (CHUNK, CHUNK), 0)
    s_i = lax.broadcasted_iota(jnp.int32, (CHUNK, CHUNK), 1)
    causal = s_i <= t_i
    lane = lax.broadcasted_iota(jnp.int32, (CHUNK, LANES), 1)
    sub = lax.broadcasted_iota(jnp.int32, (LANES, CHUNK), 0)

    gates = gt_ref[rows, :] + gb_ref[...]
    lf = jnp.minimum(gates, 0.0) - jnp.log(1.0 + jnp.exp(-jnp.abs(gates)))
    bcum = _cumsum_rows(tri_ref[...], lf)
    yield
    qraw = q_ref[rows, :]
    kraw = k_ref[rows, :]
    qc = _tiles((conv(qraw, tq_ref[...], cwq_ref[...]) * (B_DQK ** -0.5)).astype(BF16), B_DQK)
    kc32 = _tiles(conv(kraw, tk_ref[...], cwk_ref[...]), B_DQK)
    kc = [t.astype(BF16) for t in kc32]
    tq_ref[...] = qraw[CHUNK - SUBLANES:, :]
    tk_ref[...] = kraw[CHUNK - SUBLANES:, :]
    c_aug = [c_ref[j] for j in heads]
    qk = [_dot_nt(qc[j], kc[j]) for j in heads]
    qcs = [_dot(qc[j], c_aug[j].astype(BF16)) for j in heads]
    yield

    gates_t = gates.T
    bcum_t = bcum.T
    ig_col = [jnp.sum(jnp.where(lane == j, gates, 0.0), axis=1, keepdims=True) for j in heads]
    b_col = [jnp.sum(jnp.where(lane == B_HEADS + j, bcum, 0.0), axis=1, keepdims=True) for j in heads]
    ig_row = [jnp.sum(jnp.where(sub == j, gates_t, 0.0), axis=0, keepdims=True) for j in heads]
    b_row = [jnp.sum(jnp.where(sub == B_HEADS + j, bcum_t, 0.0), axis=0, keepdims=True) for j in heads]
    yield

    ones = jnp.ones((CHUNK, LANES), BF16)
    v_aug = [jnp.concatenate([t.astype(BF16), ones], axis=1) for t in _tiles(v_ref[rows, :], B_DV)]
    m_prev = [m_ref[j][0:1, 0:1] for j in heads]
    log_w = [jnp.where(causal, b_col[j] - b_row[j] + ig_row[j], NEG_BIG) for j in heads]
    log_inter = [b_col[j] + m_prev[j] for j in heads]
    m_t = [jnp.maximum(log_inter[j], jnp.max(log_w[j], axis=-1, keepdims=True)) for j in heads]
    scores = [(qk[j] * jnp.exp(log_w[j] - m_t[j])).astype(BF16) for j in heads]
    numden = [_dot(scores[j], v_aug[j]) + jnp.exp(log_inter[j] - m_t[j]) * qcs[j] for j in heads]
    yield

    b_end = [b_col[j][CHUNK - 1:CHUNK, :] for j in heads]
    log_s = [b_end[j] - b_col[j] + ig_col[j] for j in heads]
    m_new = [jnp.maximum(b_end[j] + m_prev[j], jnp.max(log_s[j], axis=0, keepdims=True)) for j in heads]
    kw = [(kc32[j] * jnp.exp(log_s[j] - m_new[j])).T.astype(BF16) for j in heads]
    for j in heads:
        c_ref[j] = jnp.exp(b_end[j] + m_prev[j] - m_new[j]) * c_aug[j] + _dot(kw[j], v_aug[j])
        m_ref[j] = jnp.broadcast_to(m_new[j], (SUBLANES, LANES))
    yield

    hid = [numden[j][:, :B_DV] / jnp.maximum(jnp.abs(numden[j][:, B_DV:B_DV + 1]), jnp.exp(-m_t[j]))
           for j in heads]
    hc = [hid[j] - jnp.mean(hid[j], axis=-1, keepdims=True) for j in heads]
    hn = [hc[j] * lax.rsqrt(jnp.mean(hc[j] * hc[j], axis=-1, keepdims=True) + NORM_EPS) for j in heads]
    y = jnp.concatenate(hn, axis=1) * ng_ref[...] * _sigmoid(og_ref[rows, :]) * _silu(z_ref[rows, :])
    o_ref[rows, cols] = y.astype(o_ref.dtype)


def _rwkv_chunk(rows, cols, r_ref, k_ref, v_ref, z_ref, lr_ref, mur_ref, muk_ref, muv_ref, mulr_ref, w0_ref, a0_ref,
                kk_ref, ka_ref, rk_ref, lng_ref, lnb_ref, w2h_ref, w2l_ref, tri_ref, ones_ref,
                o_ref, st_ref, tr_ref, tk_ref, tv_ref, tlr_ref):
    npairs = BRANCH_WIDTH // LANES
    pairs = range(npairs)

    def lerp(x_ref, tail_ref, mu_ref):
        x = x_ref[rows, :]
        out = x + (_shift_rows(x, tail_ref[...], 1) - x) * mu_ref[...]
        tail_ref[...] = x[CHUNK - SUBLANES:, :]
        return out

    tri = tri_ref[...]
    ones_bd = ones_ref[...]
    lane = lax.broadcasted_iota(jnp.int32, (CHUNK, LANES), 1)
    t_i = lax.broadcasted_iota(jnp.int32, (CHUNK, LANES), 0)
    s_i = lane % C_DH
    head0 = lane < C_DH
    strict = s_i < t_i
    incl = s_i <= t_i
    bd_r = lax.broadcasted_iota(jnp.int32, (LANES, LANES), 0) // C_DH
    bd_c = lax.broadcasted_iota(jnp.int32, (LANES, LANES), 1) // C_DH
    blockdiag = bd_r == bd_c

    def stack(x):
        return jnp.concatenate([jnp.where(head0, x, 0.0), jnp.where(head0, 0.0, x)], axis=0)

    def gsum(x):
        res = _dot(jnp.concatenate(_tiles(x.astype(BF16)), axis=0), ones_bd)
        return jnp.concatenate([res[p * CHUNK:(p + 1) * CHUNK] for p in pairs], axis=1)

    lr = lerp(lr_ref, tlr_ref, mulr_ref)
    lr = jnp.where(head0, jnp.tanh(lr), lr)
    lr_hi = lr.astype(BF16)
    lr_lo = (lr - lr_hi.astype(F32)).astype(BF16)
    up2 = _dot(jnp.concatenate([lr_hi, lr_lo], axis=0), w2h_ref[...])
    up_lo = _dot(lr_hi, w2l_ref[...])
    k0 = lerp(k_ref, tk_ref, muk_ref)
    kk = k0 * kk_ref[...]
    kk_ss = gsum(kk * kk)
    yield
    up = up2[:CHUNK] + up2[CHUNK:] + up_lo
    lw = -C_DECAY_MAX * _sigmoid(w0_ref[...] + up[:, :BRANCH_WIDTH])
    cw = _cumsum_rows(tri, lw, pieces=2)
    yield
    r = lerp(r_ref, tr_ref, mur_ref)
    v = lerp(v_ref, tv_ref, muv_ref)
    a = _sigmoid(a0_ref[...] + up[:, BRANCH_WIDTH:])
    kk = kk / jnp.maximum(jnp.sqrt(kk_ss), 1e-12)
    k = k0 * (1.0 + (a - 1.0) * ka_ref[...])
    b = kk * a
    bonus_s = gsum(r * k * rk_ref[...])
    e_c = jnp.exp(cw)
    e_n = jnp.exp(-cw)
    at = _tiles(-kk * jnp.exp(cw - lw))
    rt = _tiles(r * e_c)
    bt = _tiles(b * e_n)
    kt = _tiles(k * e_n)
    vp = _tiles(v)
    decay = _tiles(e_c[CHUNK - 1:CHUNK, :])
    st = [st_ref[p] for p in pairs]

    ar = [jnp.concatenate([at[p], rt[p]], axis=0).astype(BF16) for p in pairs]
    bk = [jnp.concatenate([stack(bt[p]), stack(kt[p])], axis=0).astype(BF16) for p in pairs]
    pm = [_dot_nt(ar[p], bk[p]) for p in pairs]
    a_s = [_dot_nt(ar[p], st[p].astype(BF16)) for p in pairs]
    yield
    v_st = [stack(vp[p]).astype(BF16) for p in pairs]
    l_ak = [jnp.where(strict, pm[p][:CHUNK, LANES:], 0.0).astype(BF16) for p in pairs]
    u = [a_s[p][:CHUNK] + _dot(l_ak[p], v_st[p]) for p in pairs]
    yield
    mk = [jnp.where(strict, pm[p][:CHUNK, :LANES], 0.0) for p in pairs]
    steps = CHUNK.bit_length() - 1
    for it in range(steps - 1):
        res = [_dot(mk[p].astype(BF16),
                    jnp.concatenate([stack(u[p]), stack(mk[p])], axis=1).astype(BF16)) for p in pairs]
        yield
        u = [u[p] + res[p][:, :LANES] for p in pairs]
        mk = [res[p][:, LANES:] for p in pairs]
    u = [u[p] + _dot(mk[p].astype(BF16), stack(u[p]).astype(BF16)) for p in pairs]
    yield

    rbk = [jnp.concatenate([jnp.where(incl, pm[p][CHUNK:, :LANES], 0.0),
                            jnp.where(incl, pm[p][CHUNK:, LANES:], 0.0)], axis=1).astype(BF16) for p in pairs]
    y = [a_s[p][CHUNK:] + _dot(rbk[p], jnp.concatenate([stack(u[p]).astype(BF16), v_st[p]], axis=0))
         for p in pairs]
    uvt = [jnp.concatenate([u[p], vp[p]], axis=0).T.astype(BF16) for p in pairs]
    bkt = [jnp.concatenate([bt[p], kt[p]], axis=0).astype(BF16) for p in pairs]
    for p in pairs:
        st_ref[p] = (st[p] + jnp.where(blockdiag, _dot(uvt[p], bkt[p]), 0.0)) * decay[p]
    yield

    y = jnp.concatenate(y, axis=1)
    mean = gsum(y) * (1.0 / C_DH)
    yield
    yc = y - mean
    var = gsum(yc * yc) * (1.0 / C_DH)
    yield
    yn = yc * lax.rsqrt(var + C_LN_EPS) * lng_ref[...] + lnb_ref[...]
    o_ref[rows, cols] = ((yn + bonus_s * v) * _silu(z_ref[rows, :])).astype(o_ref.dtype)


N_A_IN, N_B_IN, N_C_IN = 8, 11, 20
N_A_SCR, N_B_SCR, N_C_SCR = 1, 4, 5


def _mixers_kernel(*refs, chunks):
    refs = list(refs)

    def take(n):
        out = refs[:n]
        del refs[:n]
        return out

    a_in, b_in, c_in = take(N_A_IN), take(N_B_IN), take(N_C_IN)
    (y_ref,) = take(1)
    a_scr, b_scr, c_scr = take(N_A_SCR), take(N_B_SCR), take(N_C_SCR)

    @pl.when(pl.program_id(1) == 0)
    def _():
        for ref in a_scr + b_scr + c_scr:
            ref[...] = jnp.zeros_like(ref)

    def chunk(c, carry):
        rows = pl.ds(pl.multiple_of(c * CHUNK, CHUNK), CHUNK)
        w = BRANCH_WIDTH
        live = [_rwkv_chunk(rows, slice(2 * w, 3 * w), *c_in, y_ref, *c_scr),
                _hgrn2_chunk(rows, slice(0, w), *a_in, y_ref, *a_scr),
                _mlstm_chunk(rows, slice(w, 2 * w), *b_in, y_ref, *b_scr)]
        while live:
            for gen in list(live):
                if next(gen, "done") == "done":
                    live.remove(gen)
        return carry

    lax.fori_loop(0, chunks, chunk, 0)


def _mixers(proj, a_par, b_par, c_par, bsz, t_pad, tb):
    m = proj.shape[0]
    nt = t_pad // tb
    w = BRANCH_WIDTH
    qw = B_HEADS * B_DQK

    def col(off, width):
        return pl.BlockSpec((tb, width), lambda b, t: (b * nt + t, off // width))

    def whole(a):
        return pl.BlockSpec(a.shape, lambda b, t: (0,) * a.ndim)

    a_cols = [col(A_Q, w), col(A_F, w), col(A_I, w), col(A_Z, w)]
    b_cols = [col(B_Q, qw), col(B_K, qw), col(B_V, w), col(B_O, w), col(B_Z, w), col(B_G, LANES)]
    c_cols = [col(C_R, w), col(C_K, w), col(C_V, w), col(C_Z, w), col(C_LR, LANES)]
    conv_w = b_par[0]
    b_specs = [pl.BlockSpec((B_CONV, qw), lambda b, t: (0, 0)),
               pl.BlockSpec((B_CONV, qw), lambda b, t: (0, 1))] + [whole(a) for a in b_par[1:]]
    operands = ([proj] * len(a_cols) + list(a_par) + [proj] * len(b_cols) + [conv_w] + list(b_par)
                + [proj] * len(c_cols) + list(c_par))
    in_specs = (a_cols + [whole(a) for a in a_par] + b_cols + b_specs + c_cols + [whole(a) for a in c_par])
    assert len(a_cols) + len(a_par) == N_A_IN and len(b_cols) + len(b_specs) == N_B_IN
    assert len(c_cols) + len(c_par) == N_C_IN
    return pl.pallas_call(
        functools.partial(_mixers_kernel, chunks=tb // CHUNK),
        grid=(bsz, nt),
        in_specs=in_specs,
        out_specs=pl.BlockSpec((tb, N_BRANCH * w), lambda b, t: (b * nt + t, 0)),
        out_shape=jax.ShapeDtypeStruct((m, N_BRANCH * w), BF16),
        scratch_shapes=[pltpu.VMEM((A_HEADS, LANES, LANES), F32),
                        pltpu.VMEM((B_HEADS, B_DQK, B_DV + LANES), F32),
                        pltpu.VMEM((B_HEADS, SUBLANES, LANES), F32),
                        pltpu.VMEM((SUBLANES, qw), F32),
                        pltpu.VMEM((SUBLANES, qw), F32),
                        pltpu.VMEM((w // LANES, LANES, LANES), F32),
                        pltpu.VMEM((SUBLANES, w), F32),
                        pltpu.VMEM((SUBLANES, w), F32),
                        pltpu.VMEM((SUBLANES, w), F32),
                        pltpu.VMEM((SUBLANES, LANES), F32)],
        compiler_params=_params(("arbitrary", "arbitrary")),
        name="mixers",
    )(*operands)


def _post_kernel(y_ref, g_ref, wbr_ref, wout_ref, h_ref, gn_ref,
                 o_ref, hg_ref, rs_ref, mg_ref, ss_ref, *, nsub, tn):
    j = pl.program_id(2)

    @pl.when(j == 0)
    def _():
        ss_ref[...] = jnp.zeros_like(ss_ref)

    @pl.when(j < nsub)
    def _():
        w = BRANCH_WIDTH
        acc = _sigmoid(g_ref[:, :tn]) * _dot(y_ref[:, :w], wbr_ref[0])
        for b in range(1, N_BRANCH):
            acc = acc + _sigmoid(g_ref[:, b * tn:(b + 1) * tn]) * _dot(y_ref[:, b * w:(b + 1) * w], wbr_ref[b])
        mg_ref[j] = acc.astype(BF16)

    @pl.when(j >= nsub)
    def _():
        new = h_ref[...]
        for kk in range(nsub):
            new = new + _dot(mg_ref[kk], wout_ref[kk * tn:(kk + 1) * tn, :])
        row = lax.broadcasted_iota(jnp.int32, new.shape, 0)
        keep = jnp.logical_or(pl.program_id(1) > 0, row >= PAD_FRONT)
        new = jnp.where(keep, new, 0.0)
        o_ref[...] = new
        hg_ref[...] = (new * gn_ref[...]).astype(hg_ref.dtype)
        ss_ref[...] += jnp.sum(new * new, axis=-1, keepdims=True)

    @pl.when(j == 2 * nsub - 1)
    def _():
        rs_ref[...] = lax.rsqrt(ss_ref[...] * (1.0 / (nsub * tn)) + NORM_EPS)


def _post(y, proj, w_br, w_out, h, g_next, layer, bsz, t_pad, tm, tn):
    m, d = h.shape
    nt = t_pad // tm
    nsub = d // tn
    assert tn == GATE_TILE and d == D_GATE
    y_spec = pl.BlockSpec((tm, N_BRANCH * BRANCH_WIDTH), lambda b, t, j: (b * nt + t, 0))
    gate_spec = pl.BlockSpec((tm, N_BRANCH * tn),
                             lambda b, t, j: (b * nt + t, G_BASE // (N_BRANCH * tn) + jnp.minimum(j, nsub - 1)))

    def out_col(j):
        return jnp.maximum(j - nsub, 0)

    return pl.pallas_call(
        functools.partial(_post_kernel, nsub=nsub, tn=tn),
        grid=(bsz, nt, 2 * nsub),
        in_specs=[y_spec, gate_spec,
                  pl.BlockSpec((None, 3, BRANCH_WIDTH, tn), lambda b, t, j: (layer, 0, 0, jnp.minimum(j, nsub - 1))),
                  pl.BlockSpec((None, d, tn), lambda b, t, j: (layer, 0, out_col(j))),
                  pl.BlockSpec((tm, tn), lambda b, t, j: (b * nt + t, out_col(j))),
                  pl.BlockSpec((1, tn), lambda b, t, j: (0, out_col(j)))],
        out_specs=[pl.BlockSpec((tm, tn), lambda b, t, j: (b * nt + t, out_col(j))),
                   pl.BlockSpec((tm, tn), lambda b, t, j: (b * nt + t, out_col(j))),
                   pl.BlockSpec((tm, LANES), lambda b, t, j: (b * nt + t, 0))],
        out_shape=[jax.ShapeDtypeStruct((m, d), F32), jax.ShapeDtypeStruct((m, d), BF16),
                   jax.ShapeDtypeStruct((m, LANES), F32)],
        scratch_shapes=[pltpu.VMEM((nsub, tm, tn), BF16), pltpu.VMEM((tm, LANES), F32)],
        input_output_aliases={4: 0},
        compiler_params=_params(("arbitrary", "arbitrary", "arbitrary")),
        name="post",
    )(y, proj, w_br, w_out, h, g_next)


SRC_B_G = B_O + BRANCH_WIDTH
SRC_C_LR = SRC_B_G + 2 * B_HEADS + 4 * BRANCH_WIDTH
N_IN = SRC_C_LR + LANES + BRANCH_WIDTH + N_BRANCH * D_GATE
RG_TILE = 512
RG_SRC_TILES = -(-N_IN // RG_TILE)
RG_ROWS = 256
RG_RUNS = ((0, SRC_B_G // RG_TILE, 0),
           (B_Z // RG_TILE, C_Z // RG_TILE, SRC_B_G + 2 * B_HEADS - B_Z),
           (C_Z // RG_TILE, B_G // RG_TILE, SRC_C_LR + LANES - C_Z))


def _regroup_kernel(x_ref, o_ref, t2_ref, g_ref, c_ref):
    s = pl.program_id(1)
    d = t2_ref.shape[0]

    @pl.when(s >= 1)
    def _():
        t2_ref[:, :RG_TILE] = t2_ref[:, RG_TILE:]

    @pl.when(s < RG_SRC_TILES)
    def _():
        for r in range(RG_TILE // LANES):
            t2_ref[:, RG_TILE + r * LANES:RG_TILE + (r + 1) * LANES] = x_ref[r * LANES:(r + 1) * LANES, :].T

    @pl.when(s == SRC_B_G // RG_TILE)
    def _():
        g_ref[...] = t2_ref[:, RG_TILE:RG_TILE + LANES]

    @pl.when(s == SRC_C_LR // RG_TILE)
    def _():
        c_ref[...] = t2_ref[:, RG_TILE:RG_TILE + 2 * LANES]

    def emit(shift):
        base = shift // LANES * LANES
        for r in range(0, d, RG_ROWS):
            if shift == base:
                tile = t2_ref[r:r + RG_ROWS, base:base + RG_TILE]
            else:
                tile = t2_ref[r:r + RG_ROWS, base:base + RG_TILE + LANES][:, shift - base:shift - base + RG_TILE]
            o_ref[r:r + RG_ROWS, :] = tile.astype(BF16)

    t = s - 1
    for first, end, shift in RG_RUNS:
        pl.when(jnp.logical_and(t >= first, t < end))(functools.partial(emit, shift))

    @pl.when(t == B_G // RG_TILE)
    def _():
        lane = lax.broadcasted_iota(jnp.int32, (RG_ROWS, LANES), 1)
        off = SRC_C_LR % RG_TILE
        for r in range(0, d, RG_ROWS):
            rows = slice(r, r + RG_ROWS)
            o_ref[rows, :LANES] = jnp.where(lane < 2 * B_HEADS, g_ref[rows, :], 0.0).astype(BF16)
            o_ref[rows, LANES:2 * LANES] = c_ref[rows, :][:, off:off + LANES].astype(BF16)
            o_ref[rows, 2 * LANES:] = jnp.zeros((RG_ROWS, RG_TILE - 2 * LANES), BF16)


def _gate_tile_position(t):
    first = G_BASE // RG_TILE
    per_branch = D_GATE // RG_TILE
    g = t - first
    moved = first + N_BRANCH * (g % per_branch) + g // per_branch
    return jnp.where(jnp.logical_and(g >= 0, g < N_BRANCH * per_branch), moved, t)


def _regroup_w_in(w_in):
    depth, d, n_in = w_in.shape
    assert n_in == N_IN and d % RG_ROWS == 0 and N_PROJ == (B_G // RG_TILE + 1) * RG_TILE
    assert C_LR == B_G + LANES and SRC_C_LR % RG_TILE + LANES <= 2 * LANES and RG_TILE == GATE_TILE
    return pl.pallas_call(
        _regroup_kernel,
        grid=(depth, RG_SRC_TILES + 1),
        in_specs=[pl.BlockSpec((None, RG_TILE, d), lambda l, s: (l, jnp.minimum(s, RG_SRC_TILES - 1), 0))],
        out_specs=pl.BlockSpec((None, d, RG_TILE), lambda l, s: (l, 0, _gate_tile_position(jnp.maximum(s - 1, 0)))),
        out_shape=jax.ShapeDtypeStruct((depth, d, N_PROJ), BF16),
        scratch_shapes=[pltpu.VMEM((d, 2 * RG_TILE), F32), pltpu.VMEM((d, LANES), F32),
                        pltpu.VMEM((d, 2 * LANES), F32)],
        compiler_params=_params(("arbitrary", "arbitrary")),
        name="regroup_w_in",
    )(jnp.swapaxes(w_in, 1, 2))


def _rwkv_up_weights(w_up, a_up):
    top = jnp.concatenate([w_up, jnp.zeros_like(w_up)], axis=-1)
    bot = jnp.concatenate([jnp.zeros_like(a_up), a_up], axis=-1)
    w2 = jnp.concatenate([top, bot], axis=1)
    hi = w2.astype(BF16)
    lo = (w2 - hi.astype(F32)).astype(BF16)
    return hi, lo


def kernel(x, meta_tokens, norm_g, w_in, hgrn_lb_logits, hgrn_norm_g, mlstm_conv, mlstm_ig_b, mlstm_fg_b,
           mlstm_norm_g, rwkv_mu, rwkv_w0, rwkv_w_up, rwkv_a0, rwkv_a_up, rwkv_k_k, rwkv_k_a, rwkv_r_k,
           rwkv_ln_g, rwkv_ln_b, w_br, w_out, final_norm_g):
    bsz, seq, d = x.shape
    depth = w_in.shape[0]
    t_pad = PAD_FRONT + N_META + seq
    m = bsz * t_pad
    assert seq % (2 * CHUNK) == 0 and t_pad % CHUNK == 0

    tb = _divisor_tile(t_pad, 320, CHUNK)
    tm_in = _divisor_tile(m, 3328, LANES)
    tm_row = _divisor_tile(m, 640, LANES)
    tm_post = _divisor_tile(t_pad, 1040, 2 * SUBLANES)
    assert tm_post >= PAD_FRONT

    w_in_p = _regroup_w_in(w_in)
    w_br_b = w_br.astype(BF16)
    w_out_b = w_out.astype(BF16)
    p = jax.nn.softmax(hgrn_lb_logits.astype(F32), axis=0)
    lower_bounds = jnp.cumsum(p, axis=0) - p[0]
    gate_b = jnp.concatenate([mlstm_ig_b, mlstm_fg_b, jnp.zeros((depth, LANES - 2 * B_HEADS), F32)], axis=-1)
    w2h, w2l = _rwkv_up_weights(rwkv_w_up, rwkv_a_up)
    w = BRANCH_WIDTH
    mu_r, mu_k, mu_v, mu_lr = rwkv_mu[:, :w], rwkv_mu[:, w:2 * w], rwkv_mu[:, 2 * w:3 * w], rwkv_mu[:, 3 * w:]
    tri = jnp.asarray(np.tril(np.ones((CHUNK, CHUNK), np.float32)), BF16)
    levels = jnp.asarray(_hgrn2_levels())
    lane_head = np.arange(LANES) // C_DH
    ones_bd = jnp.asarray((lane_head[:, None] == lane_head[None, :]).astype(np.float32), BF16)

    meta = jnp.broadcast_to(meta_tokens[None].astype(F32), (bsz, N_META, d))
    h = jnp.concatenate([jnp.zeros((bsz, PAD_FRONT, d), F32), meta, x.astype(F32)], axis=1).reshape(m, d)

    hg = _rmsnorm(h, norm_g[0][None, :], BF16, tm_row)
    rs = jnp.ones((m, LANES), F32)
    for l in range(depth):
        row = lambda a: a[l][None, :]
        proj = _inproj(hg, rs, w_in_p, l, tm_in, 512)
        a_par = (row(lower_bounds), row(hgrn_norm_g), tri, levels)
        b_par = (mlstm_conv[l], row(gate_b), row(mlstm_norm_g), tri)
        c_par = (row(mu_r), row(mu_k), row(mu_v), row(mu_lr), row(rwkv_w0), row(rwkv_a0), row(rwkv_k_k),
                 row(rwkv_k_a), row(rwkv_r_k), row(rwkv_ln_g), row(rwkv_ln_b), w2h[l], w2l[l], tri, ones_bd)
        y = _mixers(proj, a_par, b_par, c_par, bsz, t_pad, tb)
        g_next = norm_g[min(l + 1, depth - 1)][None, :]
        h, hg, rs = _post(y, proj, w_br_b, w_out_b, h, g_next, l, bsz, t_pad, tm_post, GATE_TILE)

    return _final_norm(h, final_norm_g[None, :], bsz, t_pad, seq)
```

```python
import functools

import numpy as np
import jax
import jax.numpy as jnp
from jax import lax
from jax.experimental import pallas as pl
from jax.experimental.pallas import tpu as pltpu

F32 = jnp.float32
BF16 = jnp.bfloat16

N_META = 16
CHUNK = 64
NORM_EPS = 1e-6
NEG_BIG = -1e30
F_FLOOR = 1e-12
BRANCH_WIDTH = 1024
A_HEADS, A_DK = 8, 128
B_HEADS, B_DQK, B_DV, B_CONV = 4, 128, 256, 4
C_HEADS, C_DH, C_RANK = 16, 64, 64
C_LN_EPS = 64e-5
C_DECAY_MAX = float(np.exp(-0.5))

LANES = 128
SUBLANES = 8
PAD_FRONT = 2 * CHUNK - N_META
VMEM_LIMIT = 56 * 1024 * 1024

A_Q, A_F, A_I, A_Z = 0, 1024, 2048, 3072
B_Q, B_K, B_V, B_O, B_Z = 4096, 4608, 5120, 6144, 7168
C_R, C_K, C_V, C_Z = 8192, 9216, 10240, 11264
G_BASE = 12288
N_BRANCH = 3
D_GATE = 2048
GATE_TILE = 512
B_G = 18432
C_LR = 18560
N_PROJ = 18944


def _divisor_tile(n, target, mult):
    best = None
    for d in range(mult, min(n, target) + 1, mult):
        if n % d == 0:
            best = d
    if best is None:
        raise ValueError(f"no tile for {n} (multiple of {mult}, <= {target})")
    return best


def _params(sem):
    return pltpu.CompilerParams(dimension_semantics=sem, vmem_limit_bytes=VMEM_LIMIT)


def _sigmoid(x):
    return 0.5 * jnp.tanh(0.5 * x) + 0.5


def _silu(x):
    return x * _sigmoid(x)


def _dot(a, b):
    return jnp.dot(a, b, preferred_element_type=F32)


def _dot_nt(a, b):
    return lax.dot_general(a, b, (((1,), (1,)), ((), ())), preferred_element_type=F32)


def _split(x, pieces):
    out = []
    for _ in range(pieces - 1):
        part = x.astype(BF16)
        out.append(part)
        x = x - part.astype(F32)
    return out + [x.astype(BF16)]


def _cumsum_rows(tri, x, pieces=3):
    w = x.shape[1]
    r = _dot(tri, jnp.concatenate(_split(x, pieces), axis=1))
    acc = r[:, (pieces - 1) * w:]
    for i in range(pieces - 2, -1, -1):
        acc = acc + r[:, i * w:(i + 1) * w]
    return acc


def _row_bcast(x, r, n):
    return jnp.broadcast_to(x[r:r + 1, :], (n, x.shape[1]))


def _shift_rows(x, tail, j):
    rolled = pltpu.roll(x, j, 0)
    tail_r = pltpu.roll(tail, j, 0)
    row = lax.broadcasted_iota(jnp.int32, tail.shape, 0)
    first = jnp.where(row < j, tail_r, rolled[:SUBLANES])
    return jnp.concatenate([first, rolled[SUBLANES:]], axis=0)


def _tiles(x, w=LANES):
    return [x[:, i * w:(i + 1) * w] for i in range(x.shape[1] // w)]


def _rmsnorm_kernel(x_ref, g_ref, o_ref):
    x = x_ref[...]
    ms = jnp.mean(x * x, axis=-1, keepdims=True)
    o_ref[...] = (x * lax.rsqrt(ms + NORM_EPS) * g_ref[...]).astype(o_ref.dtype)


def _rmsnorm(h, g, out_dtype, tm):
    m, d = h.shape
    return pl.pallas_call(
        _rmsnorm_kernel,
        grid=(m // tm,),
        in_specs=[pl.BlockSpec((tm, d), lambda i: (i, 0)), pl.BlockSpec((1, d), lambda i: (0, 0))],
        out_specs=pl.BlockSpec((tm, d), lambda i: (i, 0)),
        out_shape=jax.ShapeDtypeStruct((m, d), out_dtype),
        compiler_params=_params(("arbitrary",)),
        name="rmsnorm",
    )(h, g)


def _final_norm_kernel(x_ref, g_ref, o_ref):
    x = x_ref[...]
    ms = jnp.mean(x * x, axis=-1, keepdims=True)
    o_ref[0] = x * lax.rsqrt(ms + NORM_EPS) * g_ref[...]


def _final_norm(h, g, bsz, t_pad, seq):
    d = h.shape[1]
    tm = CHUNK * 2
    per_b = t_pad // tm
    return pl.pallas_call(
        _final_norm_kernel,
        grid=(bsz, seq // tm),
        in_specs=[pl.BlockSpec((tm, d), lambda b, i: (b * per_b + i + 1, 0)),
                  pl.BlockSpec((1, d), lambda b, i: (0, 0))],
        out_specs=pl.BlockSpec((1, tm, d), lambda b, i: (b, i, 0)),
        out_shape=jax.ShapeDtypeStruct((bsz, seq, d), F32),
        compiler_params=_params(("arbitrary", "arbitrary")),
        name="final_norm",
    )(h, g)


def _inproj_kernel(x_ref, rs_ref, w_ref, o_ref):
    o_ref[...] = _dot(x_ref[...], w_ref[...]) * rs_ref[:, :1]


def _inproj(hg, rs, w, layer, tm, tn):
    m, d = hg.shape
    n = w.shape[2]
    return pl.pallas_call(
        _inproj_kernel,
        grid=(m // tm, n // tn),
        in_specs=[pl.BlockSpec((tm, d), lambda i, j: (i, 0)),
                  pl.BlockSpec((tm, LANES), lambda i, j: (i, 0)),
                  pl.BlockSpec((None, d, tn), lambda i, j: (layer, 0, j))],
        out_specs=pl.BlockSpec((tm, tn), lambda i, j: (i, j)),
        out_shape=jax.ShapeDtypeStruct((m, n), F32),
        compiler_params=_params(("arbitrary", "arbitrary")),
        name="inproj",
    )(hg, rs, w)


def _hgrn2_levels():
    t = np.arange(CHUNK)[:, None]
    s = np.arange(CHUNK)[None, :]
    lv = np.zeros((CHUNK, CHUNK), np.int32)
    lv[(t // 4 == s // 4) & (s <= t)] = 1
    for lid, sz in ((2, 4), (3, 8), (4, 16), (5, 32)):
        lv[(t // (2 * sz) == s // (2 * sz)) & (t % (2 * sz) >= sz) & (s % (2 * sz) < sz)] = lid
    return lv


def _hgrn2_chunk(rows, cols, q_ref, f_ref, i_ref, z_ref, lb_ref, g_ref, tri_ref, lv_ref, o_ref, st_ref):
    heads = range(A_HEADS)
    lv = lv_ref[...]
    sub = lax.broadcasted_iota(jnp.int32, (SUBLANES, BRANCH_WIDTH), 0)
    lb = lb_ref[...]
    fr = f_ref[rows, :]
    v = i_ref[rows, :]
    qs = _silu(q_ref[rows, :]) * (A_DK ** -0.5)
    sg = _sigmoid(fr)
    k = (1.0 - lb) * (1.0 - sg)
    lf = jnp.log(jnp.maximum(lb + (1.0 - lb) * sg, F_FLOOR))
    cg = _cumsum_rows(tri_ref[...], lf)
    yield
    cl = cg[CHUNK - 1:CHUNK, :]
    st = [st_ref[j] for j in heads]
    qg = _tiles((qs * jnp.exp(cg)).astype(BF16))
    o = [_dot_nt(qg[j], st[j].astype(BF16)) for j in heads]
    yield

    ref = jnp.concatenate(
        [jnp.where(sub < 4, _row_bcast(cg, 8 * g, SUBLANES), _row_bcast(cg, 8 * g + 4, SUBLANES))
         for g in range(CHUNK // SUBLANES)], axis=0)
    qt = _tiles((qs * jnp.exp(cg - ref)).astype(BF16))
    kt = _tiles((k * jnp.exp(ref - cg)).astype(BF16))
    att = [jnp.where(lv == 1, _dot_nt(qt[j], kt[j]), 0.0) for j in heads]
    yield
    row_i = lax.broadcasted_iota(jnp.int32, (CHUNK, BRANCH_WIDTH), 0)
    for lid, sz in ((2, 4), (3, 8), (4, 16), (5, 32)):
        ref = jnp.concatenate(
            [_row_bcast(cg, g * 2 * sz + sz - 1, 2 * sz) for g in range(CHUNK // (2 * sz))], axis=0)
        later = (row_i & sz) != 0
        x = _tiles((jnp.where(later, qs, k) * jnp.exp(-jnp.abs(cg - ref))).astype(BF16))
        att = [jnp.where(lv == lid, _dot_nt(x[j], x[j]), att[j]) for j in heads]
        yield

    vb = _tiles(v.astype(BF16))
    o = [o[j] + _dot(att[j].astype(BF16), vb[j]) for j in heads]
    yield
    kd = _tiles((k * jnp.exp(cl - cg)).astype(BF16))
    vt = [t.T.astype(BF16) for t in _tiles(v)]
    decay = _tiles(jnp.exp(cl))
    for j in heads:
        st_ref[j] = st[j] * decay[j] + _dot(vt[j], kd[j])
    yield
    on = [o[j] * lax.rsqrt(jnp.mean(o[j] * o[j], axis=-1, keepdims=True) + NORM_EPS) for j in heads]
    y = jnp.concatenate(on, axis=1) * g_ref[...] * _silu(z_ref[rows, :])
    o_ref[rows, cols] = y.astype(o_ref.dtype)


def _mlstm_chunk(rows, cols, q_ref, k_ref, v_ref, og_ref, z_ref, gt_ref, cwq_ref, cwk_ref, gb_ref, ng_ref, tri_ref,
                 o_ref, c_ref, m_ref, tq_ref, tk_ref):
    heads = range(B_HEADS)

    def conv(x, tail, w):
        acc = x * w[B_CONV - 1:B_CONV, :]
        for j in range(1, B_CONV):
            acc = acc + _shift_rows(x, tail, j) * w[B_CONV - 1 - j:B_CONV - j, :]
        return _silu(acc)

    t_i = lax.broadcasted_iota(jnp.int32, (CHUNK, CHUNK), 0)
    s_i = lax.broadcasted_iota(jnp.int32, (CHUNK, CHUNK), 1)
    causal = s_i <= t_i
    lane = lax.broadcasted_iota(jnp.int32, (CHUNK, LANES), 1)
    sub = lax.broadcasted_iota(jnp.int32, (LANES, CHUNK), 0)

    gates = gt_ref[rows, :] + gb_ref[...]
    lf = jnp.minimum(gates, 0.0) - jnp.log(1.0 + jnp.exp(-jnp.abs(gates)))
    bcum = _cumsum_rows(tri_ref[...], lf)
    yield
    qraw = q_ref[rows, :]
    kraw = k_ref[rows, :]
    qc = _tiles((conv(qraw, tq_ref[...], cwq_ref[...]) * (B_DQK ** -0.5)).astype(BF16), B_DQK)
    kc32 = _tiles(conv(kraw, tk_ref[...], cwk_ref[...]), B_DQK)
    kc = [t.astype(BF16) for t in kc32]
    tq_ref[...] = qraw[CHUNK - SUBLANES:, :]
    tk_ref[...] = kraw[CHUNK - SUBLANES:, :]
    c_aug = [c_ref[j] for j in heads]
    qk = [_dot_nt(qc[j], kc[j]) for j in heads]
    qcs = [_dot(qc[j], c_aug[j].astype(BF16)) for j in heads]
    yield

    gates_t = gates.T
    bcum_t = bcum.T
    ig_col = [jnp.sum(jnp.where(lane == j, gates, 0.0), axis=1, keepdims=True) for j in heads]
    b_col = [jnp.sum(jnp.where(lane == B_HEADS + j, bcum, 0.0), axis=1, keepdims=True) for j in heads]
    ig_row = [jnp.sum(jnp.where(sub == j, gates_t, 0.0), axis=0, keepdims=True) for j in heads]
    b_row = [jnp.sum(jnp.where(sub == B_HEADS + j, bcum_t, 0.0), axis=0, keepdims=True) for j in heads]
    yield

    ones = jnp.ones((CHUNK, LANES), BF16)
    v_aug = [jnp.concatenate([t.astype(BF16), ones], axis=1) for t in _tiles(v_ref[rows, :], B_DV)]
    m_prev = [m_ref[j][0:1, 0:1] for j in heads]
    log_w = [jnp.where(causal, b_col[j] - b_row[j] + ig_row[j], NEG_BIG) for j in heads]
    log_inter = [b_col[j] + m_prev[j] for j in heads]
    m_t = [jnp.maximum(log_inter[j], jnp.max(log_w[j], axis=-1, keepdims=True)) for j in heads]
    scores = [(qk[j] * jnp.exp(log_w[j] - m_t[j])).astype(BF16) for j in heads]
    numden = [_dot(scores[j], v_aug[j]) + jnp.exp(log_inter[j] - m_t[j]) * qcs[j] for j in heads]
    yield

    b_end = [b_col[j][CHUNK - 1:CHUNK, :] for j in heads]
    log_s = [b_end[j] - b_col[j] + ig_col[j] for j in heads]
    m_new = [jnp.maximum(b_end[j] + m_prev[j], jnp.max(log_s[j], axis=0, keepdims=True)) for j in heads]
    kw = [(kc32[j] * jnp.exp(log_s[j] - m_new[j])).T.astype(BF16) for j in heads]
    for j in heads:
        c_ref[j] = jnp.exp(b_end[j] + m_prev[j] - m_new[j]) * c_aug[j] + _dot(kw[j], v_aug[j])
        m_ref[j] = jnp.broadcast_to(m_new[j], (SUBLANES, LANES))
    yield

    hid = [numden[j][:, :B_DV] / jnp.maximum(jnp.abs(numden[j][:, B_DV:B_DV + 1]), jnp.exp(-m_t[j]))
           for j in heads]
    hc = [hid[j] - jnp.mean(hid[j], axis=-1, keepdims=True) for j in heads]
    hn = [hc[j] * lax.rsqrt(jnp.mean(hc[j] * hc[j], axis=-1, keepdims=True) + NORM_EPS) for j in heads]
    y = jnp.concatenate(hn, axis=1) * ng_ref[...] * _sigmoid(og_ref[rows, :]) * _silu(z_ref[rows, :])
    o_ref[rows, cols] = y.astype(o_ref.dtype)


def _rwkv_chunk(rows, cols, r_ref, k_ref, v_ref, z_ref, lr_ref, mur_ref, muk_ref, muv_ref, mulr_ref, w0_ref, a0_ref,
                kk_ref, ka_ref, rk_ref, lng_ref, lnb_ref, w2h_ref, w2l_ref, tri_ref, ones_ref,
                o_ref, st_ref, tr_ref, tk_ref, tv_ref, tlr_ref):
    npairs = BRANCH_WIDTH // LANES
    pairs = range(npairs)

    def lerp(x_ref, tail_ref, mu_ref):
        x = x_ref[rows, :]
        out = x + (_shift_rows(x, tail_ref[...], 1) - x) * mu_ref[...]
        tail_ref[...] = x[CHUNK - SUBLANES:, :]
        return out

    tri = tri_ref[...]
    ones_bd = ones_ref[...]
    lane = lax.broadcasted_iota(jnp.int32, (CHUNK, LANES), 1)
    t_i = lax.broadcasted_iota(jnp.int32, (CHUNK, LANES), 0)
    s_i = lane % C_DH
    head0 = lane < C_DH
    strict = s_i < t_i
    incl = s_i <= t_i
    bd_r = lax.broadcasted_iota(jnp.int32, (LANES, LANES), 0) // C_DH
    bd_c = lax.broadcasted_iota(jnp.int32, (LANES, LANES), 1) // C_DH
    blockdiag = bd_r == bd_c

    def stack(x):
        return jnp.concatenate([jnp.where(head0, x, 0.0), jnp.where(head0, 0.0, x)], axis=0)

    def gsum(x):
        res = _dot(jnp.concatenate(_tiles(x.astype(BF16)), axis=0), ones_bd)
        return jnp.concatenate([res[p * CHUNK:(p + 1) * CHUNK] for p in pairs], axis=1)

    lr = lerp(lr_ref, tlr_ref, mulr_ref)
    lr = jnp.where(head0, jnp.tanh(lr), lr)
    lr_hi = lr.astype(BF16)
    lr_lo = (lr - lr_hi.astype(F32)).astype(BF16)
    up2 = _dot(jnp.concatenate([lr_hi, lr_lo], axis=0), w2h_ref[...])
    up_lo = _dot(lr_hi, w2l_ref[...])
    k0 = lerp(k_ref, tk_ref, muk_ref)
    kk = k0 * kk_ref[...]
    kk_ss = gsum(kk * kk)
    yield
    up = up2[:CHUNK] + up2[CHUNK:] + up_lo
    lw = -C_DECAY_MAX * _sigmoid(w0_ref[...] + up[:, :BRANCH_WIDTH])
    cw = _cumsum_rows(tri, lw, pieces=2)
    yield
    r = lerp(r_ref, tr_ref, mur_ref)
    v = lerp(v_ref, tv_ref, muv_ref)
    a = _sigmoid(a0_ref[...] + up[:, BRANCH_WIDTH:])
    kk = kk / jnp.maximum(jnp.sqrt(kk_ss), 1e-12)
    k = k0 * (1.0 + (a - 1.0) * ka_ref[...])
    b = kk * a
    bonus_s = gsum(r * k * rk_ref[...])
    e_c = jnp.exp(cw)
    e_n = jnp.exp(-cw)
    at = _tiles(-kk * jnp.exp(cw - lw))
    rt = _tiles(r * e_c)
    bt = _tiles(b * e_n)
    kt = _tiles(k * e_n)
    vp = _tiles(v)
    decay = _tiles(e_c[CHUNK - 1:CHUNK, :])
    st = [st_ref[p] for p in pairs]

    ar = [jnp.concatenate([at[p], rt[p]], axis=0).astype(BF16) for p in pairs]
    bk = [jnp.concatenate([stack(bt[p]), stack(kt[p])], axis=0).astype(BF16) for p in pairs]
    pm = [_dot_nt(ar[p], bk[p]) for p in pairs]
    a_s = [_dot_nt(ar[p], st[p].astype(BF16)) for p in pairs]
    yield
    v_st = [stack(vp[p]).astype(BF16) for p in pairs]
    l_ak = [jnp.where(strict, pm[p][:CHUNK, LANES:], 0.0).astype(BF16) for p in pairs]
    u = [a_s[p][:CHUNK] + _dot(l_ak[p], v_st[p]) for p in pairs]
    yield
    mk = [jnp.where(strict, pm[p][:CHUNK, :LANES], 0.0) for p in pairs]
    steps = CHUNK.bit_length() - 1
    for it in range(steps - 1):
        res = [_dot(mk[p].astype(BF16),
                    jnp.concatenate([stack(u[p]), stack(mk[p])], axis=1).astype(BF16)) for p in pairs]
        yield
        u = [u[p] + res[p][:, :LANES] for p in pairs]
        mk = [res[p][:, LANES:] for p in pairs]
    u = [u[p] + _dot(mk[p].astype(BF16), stack(u[p]).astype(BF16)) for p in pairs]
    yield

    rbk = [jnp.concatenate([jnp.where(incl, pm[p][CHUNK:, :LANES], 0.0),
                            jnp.where(incl, pm[p][CHUNK:, LANES:], 0.0)], axis=1).astype(BF16) for p in pairs]
    y = [a_s[p][CHUNK:] + _dot(rbk[p], jnp.concatenate([stack(u[p]).astype(BF16), v_st[p]], axis=0))
         for p in pairs]
    uvt = [jnp.concatenate([u[p], vp[p]], axis=0).T.astype(BF16) for p in pairs]
    bkt = [jnp.concatenate([bt[p], kt[p]], axis=0).astype(BF16) for p in pairs]
    for p in pairs:
        st_ref[p] = (st[p] + jnp.where(blockdiag, _dot(uvt[p], bkt[p]), 0.0)) * decay[p]
    yield

    y = jnp.concatenate(y, axis=1)
    mean = gsum(y) * (1.0 / C_DH)
    yield
    yc = y - mean
    var = gsum(yc * yc) * (1.0 / C_DH)
    yield
    yn = yc * lax.rsqrt(var + C_LN_EPS) * lng_ref[...] + lnb_ref[...]
    o_ref[rows, cols] = ((yn + bonus_s * v) * _silu(z_ref[rows, :])).astype(o_ref.dtype)


N_A_IN, N_B_IN, N_C_IN = 8, 11, 20
N_A_SCR, N_B_SCR, N_C_SCR = 1, 4, 5


def _mixers_kernel(*refs, chunks):
    refs = list(refs)

    def take(n):
        out = refs[:n]
        del refs[:n]
        return out

    a_in, b_in, c_in = take(N_A_IN), take(N_B_IN), take(N_C_IN)
    (y_ref,) = take(1)
    a_scr, b_scr, c_scr = take(N_A_SCR), take(N_B_SCR), take(N_C_SCR)

    @pl.when(pl.program_id(1) == 0)
    def _():
        for ref in a_scr + b_scr + c_scr:
            ref[...] = jnp.zeros_like(ref)

    def chunk(c, carry):
        rows = pl.ds(pl.multiple_of(c * CHUNK, CHUNK), CHUNK)
        w = BRANCH_WIDTH
        live = [_rwkv_chunk(rows, slice(2 * w, 3 * w), *c_in, y_ref, *c_scr),
                _hgrn2_chunk(rows, slice(0, w), *a_in, y_ref, *a_scr),
                _mlstm_chunk(rows, slice(w, 2 * w), *b_in, y_ref, *b_scr)]
        while live:
            for gen in list(live):
                if next(gen, "done") == "done":
                    live.remove(gen)
        return carry

    lax.fori_loop(0, chunks, chunk, 0)


def _mixers(proj, a_par, b_par, c_par, bsz, t_pad, tb):
    m = proj.shape[0]
    nt = t_pad // tb
    w = BRANCH_WIDTH
    qw = B_HEADS * B_DQK

    def col(off, width):
        return pl.BlockSpec((tb, width), lambda b, t: (b * nt + t, off // width))

    def whole(a):
        return pl.BlockSpec(a.shape, lambda b, t: (0,) * a.ndim)

    a_cols = [col(A_Q, w), col(A_F, w), col(A_I, w), col(A_Z, w)]
    b_cols = [col(B_Q, qw), col(B_K, qw), col(B_V, w), col(B_O, w), col(B_Z, w), col(B_G, LANES)]
    c_cols = [col(C_R, w), col(C_K, w), col(C_V, w), col(C_Z, w), col(C_LR, LANES)]
    conv_w = b_par[0]
    b_specs = [pl.BlockSpec((B_CONV, qw), lambda b, t: (0, 0)),
               pl.BlockSpec((B_CONV, qw), lambda b, t: (0, 1))] + [whole(a) for a in b_par[1:]]
    operands = ([proj] * len(a_cols) + list(a_par) + [proj] * len(b_cols) + [conv_w] + list(b_par)
                + [proj] * len(c_cols) + list(c_par))
    in_specs = (a_cols + [whole(a) for a in a_par] + b_cols + b_specs + c_cols + [whole(a) for a in c_par])
    assert len(a_cols) + len(a_par) == N_A_IN and len(b_cols) + len(b_specs) == N_B_IN
    assert len(c_cols) + len(c_par) == N_C_IN
    return pl.pallas_call(
        functools.partial(_mixers_kernel, chunks=tb // CHUNK),
        grid=(bsz, nt),
        in_specs=in_specs,
        out_specs=pl.BlockSpec((tb, N_BRANCH * w), lambda b, t: (b * nt + t, 0)),
        out_shape=jax.ShapeDtypeStruct((m, N_BRANCH * w), BF16),
        scratch_shapes=[pltpu.VMEM((A_HEADS, LANES, LANES), F32),
                        pltpu.VMEM((B_HEADS, B_DQK, B_DV + LANES), F32),
                        pltpu.VMEM((B_HEADS, SUBLANES, LANES), F32),
                        pltpu.VMEM((SUBLANES, qw), F32),
                        pltpu.VMEM((SUBLANES, qw), F32),
                        pltpu.VMEM((w // LANES, LANES, LANES), F32),
                        pltpu.VMEM((SUBLANES, w), F32),
                        pltpu.VMEM((SUBLANES, w), F32),
                        pltpu.VMEM((SUBLANES, w), F32),
                        pltpu.VMEM((SUBLANES, LANES), F32)],
        compiler_params=_params(("arbitrary", "arbitrary")),
        name="mixers",
    )(*operands)


def _post_kernel(y_ref, g_ref, wbr_ref, wout_ref, h_ref, gn_ref,
                 o_ref, hg_ref, rs_ref, mg_ref, ss_ref, *, nsub, tn):
    j = pl.program_id(2)

    @pl.when(j == 0)
    def _():
        ss_ref[...] = jnp.zeros_like(ss_ref)

    @pl.when(j < nsub)
    def _():
        w = BRANCH_WIDTH
        acc = _sigmoid(g_ref[:, :tn]) * _dot(y_ref[:, :w], wbr_ref[j, 0])
        for b in range(1, N_BRANCH):
            acc = acc + _sigmoid(g_ref[:, b * tn:(b + 1) * tn]) * _dot(y_ref[:, b * w:(b + 1) * w], wbr_ref[j, b])
        mg_ref[j] = acc.astype(BF16)

    @pl.when(j >= nsub)
    def _():
        new = h_ref[...]
        for kk in range(nsub):
            new = new + _dot(mg_ref[kk], wout_ref[j - nsub, kk * tn:(kk + 1) * tn, :])
        row = lax.broadcasted_iota(jnp.int32, new.shape, 0)
        keep = jnp.logical_or(pl.program_id(1) > 0, row >= PAD_FRONT)
        new = jnp.where(keep, new, 0.0)
        o_ref[...] = new
        hg_ref[...] = (new * gn_ref[...]).astype(hg_ref.dtype)
        ss_ref[...] += jnp.sum(new * new, axis=-1, keepdims=True)

    @pl.when(j == 2 * nsub - 1)
    def _():
        rs_ref[...] = lax.rsqrt(ss_ref[...] * (1.0 / (nsub * tn)) + NORM_EPS)


def _post(y, proj, w_br, w_out, h, g_next, layer, bsz, t_pad, tm, tn):
    m, d = h.shape
    nt = t_pad // tm
    nsub = d // tn
    assert tn == GATE_TILE and d == D_GATE
    y_spec = pl.BlockSpec((tm, N_BRANCH * BRANCH_WIDTH), lambda b, t, j: (b * nt + t, 0))
    gate_spec = pl.BlockSpec((tm, N_BRANCH * tn),
                             lambda b, t, j: (b * nt + t, G_BASE // (N_BRANCH * tn) + jnp.minimum(j, nsub - 1)))

    def out_col(j):
        return jnp.maximum(j - nsub, 0)

    return pl.pallas_call(
        functools.partial(_post_kernel, nsub=nsub, tn=tn),
        grid=(bsz, nt, 2 * nsub),
        in_specs=[y_spec, gate_spec,
                  pl.BlockSpec((None, nsub, N_BRANCH, BRANCH_WIDTH, tn), lambda b, t, j: (layer, 0, 0, 0, 0),
                               pipeline_mode=pl.Buffered(1)),
                  pl.BlockSpec((None, nsub, d, tn), lambda b, t, j: (layer, 0, 0, 0),
                               pipeline_mode=pl.Buffered(1)),
                  pl.BlockSpec((tm, tn), lambda b, t, j: (b * nt + t, out_col(j))),
                  pl.BlockSpec((1, tn), lambda b, t, j: (0, out_col(j)))],
        out_specs=[pl.BlockSpec((tm, tn), lambda b, t, j: (b * nt + t, out_col(j))),
                   pl.BlockSpec((tm, tn), lambda b, t, j: (b * nt + t, out_col(j))),
                   pl.BlockSpec((tm, LANES), lambda b, t, j: (b * nt + t, 0))],
        out_shape=[jax.ShapeDtypeStruct((m, d), F32), jax.ShapeDtypeStruct((m, d), BF16),
                   jax.ShapeDtypeStruct((m, LANES), F32)],
        scratch_shapes=[pltpu.VMEM((nsub, tm, tn), BF16), pltpu.VMEM((tm, LANES), F32)],
        input_output_aliases={4: 0},
        compiler_params=_params(("arbitrary", "arbitrary", "arbitrary")),
        name="post",
    )(y, proj, w_br, w_out, h, g_next)


SRC_B_G = B_O + BRANCH_WIDTH
SRC_C_LR = SRC_B_G + 2 * B_HEADS + 4 * BRANCH_WIDTH
N_IN = SRC_C_LR + LANES + BRANCH_WIDTH + N_BRANCH * D_GATE
RG_TILE = 512
RG_SRC_TILES = -(-N_IN // RG_TILE)
RG_ROWS = 256
RG_RUNS = ((0, SRC_B_G // RG_TILE, 0),
           (B_Z // RG_TILE, C_Z // RG_TILE, SRC_B_G + 2 * B_HEADS - B_Z),
           (C_Z // RG_TILE, B_G // RG_TILE, SRC_C_LR + LANES - C_Z))


def _regroup_kernel(x_ref, o_ref, t2_ref, g_ref, c_ref):
    s = pl.program_id(1)
    d = t2_ref.shape[0]

    @pl.when(s >= 1)
    def _():
        t2_ref[:, :RG_TILE] = t2_ref[:, RG_TILE:]

    @pl.when(s < RG_SRC_TILES)
    def _():
        for r in range(RG_TILE // LANES):
            t2_ref[:, RG_TILE + r * LANES:RG_TILE + (r + 1) * LANES] = x_ref[r * LANES:(r + 1) * LANES, :].T

    @pl.when(s == SRC_B_G // RG_TILE)
    def _():
        g_ref[...] = t2_ref[:, RG_TILE:RG_TILE + LANES]

    @pl.when(s == SRC_C_LR // RG_TILE)
    def _():
        c_ref[...] = t2_ref[:, RG_TILE:RG_TILE + 2 * LANES]

    def emit(shift):
        base = shift // LANES * LANES
        for r in range(0, d, RG_ROWS):
            if shift == base:
                tile = t2_ref[r:r + RG_ROWS, base:base + RG_TILE]
            else:
                tile = t2_ref[r:r + RG_ROWS, base:base + RG_TILE + LANES][:, shift - base:shift - base + RG_TILE]
            o_ref[r:r + RG_ROWS, :] = tile.astype(BF16)

    t = s - 1
    for first, end, shift in RG_RUNS:
        pl.when(jnp.logical_and(t >= first, t < end))(functools.partial(emit, shift))

    @pl.when(t == B_G // RG_TILE)
    def _():
        lane = lax.broadcasted_iota(jnp.int32, (RG_ROWS, LANES), 1)
        off = SRC_C_LR % RG_TILE
        for r in range(0, d, RG_ROWS):
            rows = slice(r, r + RG_ROWS)
            o_ref[rows, :LANES] = jnp.where(lane < 2 * B_HEADS, g_ref[rows, :], 0.0).astype(BF16)
            o_ref[rows, LANES:2 * LANES] = c_ref[rows, :][:, off:off + LANES].astype(BF16)
            o_ref[rows, 2 * LANES:] = jnp.zeros((RG_ROWS, RG_TILE - 2 * LANES), BF16)


def _gate_tile_position(t):
    first = G_BASE // RG_TILE
    per_branch = D_GATE // RG_TILE
    g = t - first
    moved = first + N_BRANCH * (g % per_branch) + g // per_branch
    return jnp.where(jnp.logical_and(g >= 0, g < N_BRANCH * per_branch), moved, t)


def _regroup_w_in(w_in):
    depth, d, n_in = w_in.shape
    assert n_in == N_IN and d % RG_ROWS == 0 and N_PROJ == (B_G // RG_TILE + 1) * RG_TILE
    assert C_LR == B_G + LANES and SRC_C_LR % RG_TILE + LANES <= 2 * LANES and RG_TILE == GATE_TILE
    return pl.pallas_call(
        _regroup_kernel,
        grid=(depth, RG_SRC_TILES + 1),
        in_specs=[pl.BlockSpec((None, RG_TILE, d), lambda l, s: (l, jnp.minimum(s, RG_SRC_TILES - 1), 0))],
        out_specs=pl.BlockSpec((None, d, RG_TILE), lambda l, s: (l, 0, _gate_tile_position(jnp.maximum(s - 1, 0)))),
        out_shape=jax.ShapeDtypeStruct((depth, d, N_PROJ), BF16),
        scratch_shapes=[pltpu.VMEM((d, 2 * RG_TILE), F32), pltpu.VMEM((d, LANES), F32),
                        pltpu.VMEM((d, 2 * LANES), F32)],
        compiler_params=_params(("arbitrary", "arbitrary")),
        name="regroup_w_in",
    )(jnp.swapaxes(w_in, 1, 2))


def _rwkv_up_weights(w_up, a_up):
    top = jnp.concatenate([w_up, jnp.zeros_like(w_up)], axis=-1)
    bot = jnp.concatenate([jnp.zeros_like(a_up), a_up], axis=-1)
    w2 = jnp.concatenate([top, bot], axis=1)
    hi = w2.astype(BF16)
    lo = (w2 - hi.astype(F32)).astype(BF16)
    return hi, lo


def kernel(x, meta_tokens, norm_g, w_in, hgrn_lb_logits, hgrn_norm_g, mlstm_conv, mlstm_ig_b, mlstm_fg_b,
           mlstm_norm_g, rwkv_mu, rwkv_w0, rwkv_w_up, rwkv_a0, rwkv_a_up, rwkv_k_k, rwkv_k_a, rwkv_r_k,
           rwkv_ln_g, rwkv_ln_b, w_br, w_out, final_norm_g):
    bsz, seq, d = x.shape
    depth = w_in.shape[0]
    t_pad = PAD_FRONT + N_META + seq
    m = bsz * t_pad
    assert seq % (2 * CHUNK) == 0 and t_pad % CHUNK == 0

    tb = _divisor_tile(t_pad, 320, CHUNK)
    tm_in = _divisor_tile(m, 3328, LANES)
    tm_row = _divisor_tile(m, 640, LANES)
    tm_post = _divisor_tile(t_pad, 832, 2 * SUBLANES)
    assert tm_post >= PAD_FRONT

    w_in_p = _regroup_w_in(w_in)
    nsub = d // GATE_TILE
    w_br_b = w_br.astype(BF16).reshape(depth, N_BRANCH, BRANCH_WIDTH, nsub, GATE_TILE).transpose(0, 3, 1, 2, 4)
    w_out_b = w_out.astype(BF16).reshape(depth, d, nsub, GATE_TILE).transpose(0, 2, 1, 3)
    p = jax.nn.softmax(hgrn_lb_logits.astype(F32), axis=0)
    lower_bounds = jnp.cumsum(p, axis=0) - p[0]
    gate_b = jnp.concatenate([mlstm_ig_b, mlstm_fg_b, jnp.zeros((depth, LANES - 2 * B_HEADS), F32)], axis=-1)
    w2h, w2l = _rwkv_up_weights(rwkv_w_up, rwkv_a_up)
    w = BRANCH_WIDTH
    mu_r, mu_k, mu_v, mu_lr = rwkv_mu[:, :w], rwkv_mu[:, w:2 * w], rwkv_mu[:, 2 * w:3 * w], rwkv_mu[:, 3 * w:]
    tri = jnp.asarray(np.tril(np.ones((CHUNK, CHUNK), np.float32)), BF16)
    levels = jnp.asarray(_hgrn2_levels())
    lane_head = np.arange(LANES) // C_DH
    ones_bd = jnp.asarray((lane_head[:, None] == lane_head[None, :]).astype(np.float32), BF16)

    meta = jnp.broadcast_to(meta_tokens[None].astype(F32), (bsz, N_META, d))
    h = jnp.concatenate([jnp.zeros((bsz, PAD_FRONT, d), F32), meta, x.astype(F32)], axis=1).reshape(m, d)

    hg = _rmsnorm(h, norm_g[0][None, :], BF16, tm_row)
    rs = jnp.ones((m, LANES), F32)
    for l in range(depth):
        row = lambda a: a[l][None, :]
        proj = _inproj(hg, rs, w_in_p, l, tm_in, 512)
        a_par = (row(lower_bounds), row(hgrn_norm_g), tri, levels)
        b_par = (mlstm_conv[l], row(gate_b), row(mlstm_norm_g), tri)
        c_par = (row(mu_r), row(mu_k), row(mu_v), row(mu_lr), row(rwkv_w0), row(rwkv_a0), row(rwkv_k_k),
                 row(rwkv_k_a), row(rwkv_r_k), row(rwkv_ln_g), row(rwkv_ln_b), w2h[l], w2l[l], tri, ones_bd)
        y = _mixers(proj, a_par, b_par, c_par, bsz, t_pad, tb)
        g_next = norm_g[min(l + 1, depth - 1)][None, :]
        h, hg, rs = _post(y, proj, w_br_b, w_out_b, h, g_next, l, bsz, t_pad, tm_post, GATE_TILE)

    return _final_norm(h, final_norm_g[None, :], bsz, t_pad, seq)
```

```python
import functools

import numpy as np
import jax
import jax.numpy as jnp
from jax import lax
from jax.experimental import pallas as pl
from jax.experimental.pallas import tpu as pltpu

F32 = jnp.float32
BF16 = jnp.bfloat16

N_META = 16
CHUNK = 64
NORM_EPS = 1e-6
NEG_BIG = -1e30
F_FLOOR = 1e-12
BRANCH_WIDTH = 1024
A_HEADS, A_DK = 8, 128
B_HEADS, B_DQK, B_DV, B_CONV = 4, 128, 256, 4
C_HEADS, C_DH, C_RANK = 16, 64, 64
C_LN_EPS = 64e-5
C_DECAY_MAX = float(np.exp(-0.5))

LANES = 128
SUBLANES = 8
PAD_FRONT = 2 * CHUNK - N_META
VMEM_LIMIT = 56 * 1024 * 1024

A_Q, A_F, A_I, A_Z = 0, 1024, 2048, 3072
B_Q, B_K, B_V, B_O, B_Z = 4096, 4608, 5120, 6144, 7168
C_R, C_K, C_V, C_Z = 8192, 9216, 10240, 11264
G_BASE = 12288
N_BRANCH = 3
D_GATE = 2048
GATE_TILE = 512
B_G = 18432
C_LR = 18560
N_PROJ = 18944


def _divisor_tile(n, target, mult):
    best = None
    for d in range(mult, min(n, target) + 1, mult):
        if n % d == 0:
            best = d
    if best is None:
        raise ValueError(f"no tile for {n} (multiple of {mult}, <= {target})")
    return best


def _params(sem):
    return pltpu.CompilerParams(dimension_semantics=sem, vmem_limit_bytes=VMEM_LIMIT)


def _sigmoid(x):
    return 0.5 * jnp.tanh(0.5 * x) + 0.5


def _silu(x):
    return x * _sigmoid(x)


def _dot(a, b):
    return jnp.dot(a, b, preferred_element_type=F32)


def _dot_nt(a, b):
    return lax.dot_general(a, b, (((1,), (1,)), ((), ())), preferred_element_type=F32)


def _split(x, pieces):
    out = []
    for _ in range(pieces - 1):
        part = x.astype(BF16)
        out.append(part)
        x = x - part.astype(F32)
    return out + [x.astype(BF16)]


def _cumsum_rows(tri, x, pieces=3):
    w = x.shape[1]
    r = _dot(tri, jnp.concatenate(_split(x, pieces), axis=1))
    acc = r[:, (pieces - 1) * w:]
    for i in range(pieces - 2, -1, -1):
        acc = acc + r[:, i * w:(i + 1) * w]
    return acc


def _row_bcast(x, r, n):
    return jnp.broadcast_to(x[r:r + 1, :], (n, x.shape[1]))


def _shift_rows(x, tail, j):
    rolled = pltpu.roll(x, j, 0)
    tail_r = pltpu.roll(tail, j, 0)
    row = lax.broadcasted_iota(jnp.int32, tail.shape, 0)
    first = jnp.where(row < j, tail_r, rolled[:SUBLANES])
    return jnp.concatenate([first, rolled[SUBLANES:]], axis=0)


def _tiles(x, w=LANES):
    return [x[:, i * w:(i + 1) * w] for i in range(x.shape[1] // w)]


def _rmsnorm_kernel(x_ref, g_ref, o_ref):
    x = x_ref[...]
    ms = jnp.mean(x * x, axis=-1, keepdims=True)
    o_ref[...] = (x * lax.rsqrt(ms + NORM_EPS) * g_ref[...]).astype(o_ref.dtype)


def _rmsnorm(h, g, out_dtype, tm):
    m, d = h.shape
    return pl.pallas_call(
        _rmsnorm_kernel,
        grid=(m // tm,),
        in_specs=[pl.BlockSpec((tm, d), lambda i: (i, 0)), pl.BlockSpec((1, d), lambda i: (0, 0))],
        out_specs=pl.BlockSpec((tm, d), lambda i: (i, 0)),
        out_shape=jax.ShapeDtypeStruct((m, d), out_dtype),
        compiler_params=_params(("arbitrary",)),
        name="rmsnorm",
    )(h, g)


def _final_norm_kernel(x_ref, g_ref, o_ref):
    x = x_ref[...]
    ms = jnp.mean(x * x, axis=-1, keepdims=True)
    o_ref[0] = x * lax.rsqrt(ms + NORM_EPS) * g_ref[...]


def _final_norm(h, g, bsz, t_pad, seq):
    d = h.shape[1]
    tm = CHUNK * 2
    per_b = t_pad // tm
    return pl.pallas_call(
        _final_norm_kernel,
        grid=(bsz, seq // tm),
        in_specs=[pl.BlockSpec((tm, d), lambda b, i: (b * per_b + i + 1, 0)),
                  pl.BlockSpec((1, d), lambda b, i: (0, 0))],
        out_specs=pl.BlockSpec((1, tm, d), lambda b, i: (b, i, 0)),
        out_shape=jax.ShapeDtypeStruct((bsz, seq, d), F32),
        compiler_params=_params(("arbitrary", "arbitrary")),
        name="final_norm",
    )(h, g)


def _inproj_kernel(x_ref, rs_ref, w_ref, o_ref):
    o_ref[...] = _dot(x_ref[...], w_ref[...]) * rs_ref[:, :1]


def _inproj(hg, rs, w, layer, tm, tn):
    m, d = hg.shape
    n = w.shape[2]
    return pl.pallas_call(
        _inproj_kernel,
        grid=(m // tm, n // tn),
        in_specs=[pl.BlockSpec((tm, d), lambda i, j: (i, 0)),
                  pl.BlockSpec((tm, LANES), lambda i, j: (i, 0)),
                  pl.BlockSpec((None, d, tn), lambda i, j: (layer, 0, j))],
        out_specs=pl.BlockSpec((tm, tn), lambda i, j: (i, j)),
        out_shape=jax.ShapeDtypeStruct((m, n), F32),
        compiler_params=_params(("arbitrary", "arbitrary")),
        name="inproj",
    )(hg, rs, w)


def _hgrn2_levels():
    t = np.arange(CHUNK)[:, None]
    s = np.arange(CHUNK)[None, :]
    lv = np.zeros((CHUNK, CHUNK), np.int32)
    lv[(t // 4 == s // 4) & (s <= t)] = 1
    for lid, sz in ((2, 4), (3, 8), (4, 16), (5, 32)):
        lv[(t // (2 * sz) == s // (2 * sz)) & (t % (2 * sz) >= sz) & (s % (2 * sz) < sz)] = lid
    return lv


def _hgrn2_chunk(rows, cols, q_ref, f_ref, i_ref, z_ref, lb_ref, g_ref, tri_ref, lv_ref, o_ref, st_ref):
    heads = range(A_HEADS)
    lv = lv_ref[...]
    sub = lax.broadcasted_iota(jnp.int32, (SUBLANES, BRANCH_WIDTH), 0)
    lb = lb_ref[...]
    fr = f_ref[rows, :]
    v = i_ref[rows, :]
    qs = _silu(q_ref[rows, :]) * (A_DK ** -0.5)
    sg = _sigmoid(fr)
    k = (1.0 - lb) * (1.0 - sg)
    lf = jnp.log(jnp.maximum(lb + (1.0 - lb) * sg, F_FLOOR))
    cg = _cumsum_rows(tri_ref[...], lf)
    yield
    cl = cg[CHUNK - 1:CHUNK, :]
    st = [st_ref[j] for j in heads]
    qg = _tiles((qs * jnp.exp(cg)).astype(BF16))
    o = [_dot_nt(qg[j], st[j].astype(BF16)) for j in heads]
    yield

    ref = jnp.concatenate(
        [jnp.where(sub < 4, _row_bcast(cg, 8 * g, SUBLANES), _row_bcast(cg, 8 * g + 4, SUBLANES))
         for g in range(CHUNK // SUBLANES)], axis=0)
    qt = _tiles((qs * jnp.exp(cg - ref)).astype(BF16))
    kt = _tiles((k * jnp.exp(ref - cg)).astype(BF16))
    att = [jnp.where(lv == 1, _dot_nt(qt[j], kt[j]), 0.0) for j in heads]
    yield
    row_i = lax.broadcasted_iota(jnp.int32, (CHUNK, BRANCH_WIDTH), 0)
    for lid, sz in ((2, 4), (3, 8), (4, 16), (5, 32)):
        ref = jnp.concatenate(
            [_row_bcast(cg, g * 2 * sz + sz - 1, 2 * sz) for g in range(CHUNK // (2 * sz))], axis=0)
        later = (row_i & sz) != 0
        x = _tiles((jnp.where(later, qs, k) * jnp.exp(-jnp.abs(cg - ref))).astype(BF16))
        att = [jnp.where(lv == lid, _dot_nt(x[j], x[j]), att[j]) for j in heads]
        yield

    vb = _tiles(v.astype(BF16))
    o = [o[j] + _dot(att[j].astype(BF16), vb[j]) for j in heads]
    yield
    kd = _tiles((k * jnp.exp(cl - cg)).astype(BF16))
    vt = [t.T.astype(BF16) for t in _tiles(v)]
    decay = _tiles(jnp.exp(cl))
    for j in heads:
        st_ref[j] = st[j] * decay[j] + _dot(vt[j], kd[j])
    yield
    on = [o[j] * lax.rsqrt(jnp.mean(o[j] * o[j], axis=-1, keepdims=True) + NORM_EPS) for j in heads]
    y = jnp.concatenate(on, axis=1) * g_ref[...] * _silu(z_ref[rows, :])
    o_ref[rows, cols] = y.astype(o_ref.dtype)


def _mlstm_chunk(rows, cols, q_ref, k_ref, v_ref, og_ref, z_ref, gt_ref, cwq_ref, cwk_ref, gb_ref, ng_ref, tri_ref,
                 o_ref, c_ref, m_ref, tq_ref, tk_ref):
    heads = range(B_HEADS)

    def conv(x, tail, w):
        acc = x * w[B_CONV - 1:B_CONV, :]
        for j in range(1, B_CONV):
            acc = acc + _shift_rows(x, tail, j) * w[B_CONV - 1 - j:B_CONV - j, :]
        return _silu(acc)

    t_i = lax.broadcasted_iota(jnp.int32, (CHUNK, CHUNK), 0)
    s_i = lax.broadcasted_iota(jnp.int32, (CHUNK, CHUNK), 1)
    causal = s_i <= t_i
    lane = lax.broadcasted_iota(jnp.int32, (CHUNK, LANES), 1)
    sub = lax.broadcasted_iota(jnp.int32, (LANES, CHUNK), 0)

    gates = gt_ref[rows, :] + gb_ref[...]
    lf = jnp.minimum(gates, 0.0) - jnp.log(1.0 + jnp.exp(-jnp.abs(gates)))
    bcum = _cumsum_rows(tri_ref[...], lf)
    yield
    qraw = q_ref[rows, :]
    kraw = k_ref[rows, :]
    qc = _tiles((conv(qraw, tq_ref[...], cwq_ref[...]) * (B_DQK ** -0.5)).astype(BF16), B_DQK)
    kc32 = _tiles(conv(kraw, tk_ref[...], cwk_ref[...]), B_DQK)
    kc = [t.astype(BF16) for t in kc32]
    tq_ref[...] = qraw[CHUNK - SUBLANES:, :]
    tk_ref[...] = kraw[CHUNK - SUBLANES:, :]
    c_aug = [c_ref[j] for j in heads]
    qk = [_dot_nt(qc[j], kc[j]) for j in heads]
    qcs = [_dot(qc[j], c_aug[j].astype(BF16)) for j in heads]
    yield

    gates_t = gates.T
    bcum_t = bcum.T
    ig_col = [jnp.sum(jnp.where(lane == j, gates, 0.0), axis=1, keepdims=True) for j in heads]
    b_col = [jnp.sum(jnp.where(lane == B_HEADS + j, bcum, 0.0), axis=1, keepdims=True) for j in heads]
    ig_row = [jnp.sum(jnp.where(sub == j, gates_t, 0.0), axis=0, keepdims=True) for j in heads]
    b_row = [jnp.sum(jnp.where(sub == B_HEADS + j, bcum_t, 0.0), axis=0, keepdims=True) for j in heads]
    yield

    ones = jnp.ones((CHUNK, LANES), BF16)
    v_aug = [jnp.concatenate([t.astype(BF16), ones], axis=1) for t in _tiles(v_ref[rows, :], B_DV)]
    m_prev = [m_ref[j][0:1, 0:1] for j in heads]
    log_w = [jnp.where(causal, b_col[j] - b_row[j] + ig_row[j], NEG_BIG) for j in heads]
    log_inter = [b_col[j] + m_prev[j] for j in heads]
    m_t = [jnp.maximum(log_inter[j], jnp.max(log_w[j], axis=-1, keepdims=True)) for j in heads]
    scores = [(qk[j] * jnp.exp(log_w[j] - m_t[j])).astype(BF16) for j in heads]
    numden = [_dot(scores[j], v_aug[j]) + jnp.exp(log_inter[j] - m_t[j]) * qcs[j] for j in heads]
    yield

    b_end = [b_col[j][CHUNK - 1:CHUNK, :] for j in heads]
    log_s = [b_end[j] - b_col[j] + ig_col[j] for j in heads]
    m_new = [jnp.maximum(b_end[j] + m_prev[j], jnp.max(log_s[j], axis=0, keepdims=True)) for j in heads]
    kw = [(kc32[j] * jnp.exp(log_s[j] - m_new[j])).T.astype(BF16) for j in heads]
    for j in heads:
        c_ref[j] = jnp.exp(b_end[j] + m_prev[j] - m_new[j]) * c_aug[j] + _dot(kw[j], v_aug[j])
        m_ref[j] = jnp.broadcast_to(m_new[j], (SUBLANES, LANES))
    yield

    hid = [numden[j][:, :B_DV] / jnp.maximum(jnp.abs(numden[j][:, B_DV:B_DV + 1]), jnp.exp(-m_t[j]))
           for j in heads]
    hc = [hid[j] - jnp.mean(hid[j], axis=-1, keepdims=True) for j in heads]
    hn = [hc[j] * lax.rsqrt(jnp.mean(hc[j] * hc[j], axis=-1, keepdims=True) + NORM_EPS) for j in heads]
    y = jnp.concatenate(hn, axis=1) * ng_ref[...] * _sigmoid(og_ref[rows, :]) * _silu(z_ref[rows, :])
    o_ref[rows, cols] = y.astype(o_ref.dtype)


def _rwkv_chunk(rows, cols, r_ref, k_ref, v_ref, z_ref, lr_ref, mur_ref, muk_ref, muv_ref, mulr_ref, w0_ref, a0_ref,
                kk_ref, ka_ref, rk_ref, lng_ref, lnb_ref, w2h_ref, w2l_ref, tri_ref, ones_ref,
                o_ref, st_ref, tr_ref, tk_ref, tv_ref, tlr_ref):
    npairs = BRANCH_WIDTH // LANES
    pairs = range(npairs)

    def lerp(x_ref, tail_ref, mu_ref):
        x = x_ref[rows, :]
        out = x + (_shift_rows(x, tail_ref[...], 1) - x) * mu_ref[...]
        tail_ref[...] = x[CHUNK - SUBLANES:, :]
        return out

    tri = tri_ref[...]
    ones_bd = ones_ref[...]
    lane = lax.broadcasted_iota(jnp.int32, (CHUNK, LANES), 1)
    t_i = lax.broadcasted_iota(jnp.int32, (CHUNK, LANES), 0)
    s_i = lane % C_DH
    head0 = lane < C_DH
    strict = s_i < t_i
    incl = s_i <= t_i
    bd_r = lax.broadcasted_iota(jnp.int32, (LANES, LANES), 0) // C_DH
    bd_c = lax.broadcasted_iota(jnp.int32, (LANES, LANES), 1) // C_DH
    blockdiag = bd_r == bd_c

    def stack(x):
        return jnp.concatenate([jnp.where(head0, x, 0.0), jnp.where(head0, 0.0, x)], axis=0)

    def gsum(x):
        res = _dot(jnp.concatenate(_tiles(x.astype(BF16)), axis=0), ones_bd)
        return jnp.concatenate([res[p * CHUNK:(p + 1) * CHUNK] for p in pairs], axis=1)

    lr = lerp(lr_ref, tlr_ref, mulr_ref)
    lr = jnp.where(head0, jnp.tanh(lr), lr)
    lr_hi = lr.astype(BF16)
    lr_lo = (lr - lr_hi.astype(F32)).astype(BF16)
    up2 = _dot(jnp.concatenate([lr_hi, lr_lo], axis=0), w2h_ref[...])
    up_lo = _dot(lr_hi, w2l_ref[...])
    k0 = lerp(k_ref, tk_ref, muk_ref)
    kk = k0 * kk_ref[...]
    kk_ss = gsum(kk * kk)
    yield
    up = up2[:CHUNK] + up2[CHUNK:] + up_lo
    lw = -C_DECAY_MAX * _sigmoid(w0_ref[...] + up[:, :BRANCH_WIDTH])
    cw = _cumsum_rows(tri, lw, pieces=2)
    yield
    r = lerp(r_ref, tr_ref, mur_ref)
    v = lerp(v_ref, tv_ref, muv_ref)
    a = _sigmoid(a0_ref[...] + up[:, BRANCH_WIDTH:])
    kk = kk / jnp.maximum(jnp.sqrt(kk_ss), 1e-12)
    k = k0 * (1.0 + (a - 1.0) * ka_ref[...])
    b = kk * a
    bonus_s = gsum(r * k * rk_ref[...])
    e_c = jnp.exp(cw)
    e_n = jnp.exp(-cw)
    at = _tiles(-kk * jnp.exp(cw - lw))
    rt = _tiles(r * e_c)
    bt = _tiles(b * e_n)
    kt = _tiles(k * e_n)
    vp = _tiles(v)
    decay = _tiles(e_c[CHUNK - 1:CHUNK, :])
    st = [st_ref[p] for p in pairs]

    ar = [jnp.concatenate([at[p], rt[p]], axis=0).astype(BF16) for p in pairs]
    bk = [jnp.concatenate([stack(bt[p]), stack(kt[p])], axis=0).astype(BF16) for p in pairs]
    pm = [_dot_nt(ar[p], bk[p]) for p in pairs]
    a_s = [_dot_nt(ar[p], st[p].astype(BF16)) for p in pairs]
    yield
    v_st = [stack(vp[p]).astype(BF16) for p in pairs]
    l_ak = [jnp.where(strict, pm[p][:CHUNK, LANES:], 0.0).astype(BF16) for p in pairs]
    u = [a_s[p][:CHUNK] + _dot(l_ak[p], v_st[p]) for p in pairs]
    yield
    mk = [jnp.where(strict, pm[p][:CHUNK, :LANES], 0.0) for p in pairs]
    steps = CHUNK.bit_length() - 1
    for it in range(steps - 1):
        res = [_dot(mk[p].astype(BF16),
                    jnp.concatenate([stack(u[p]), stack(mk[p])], axis=1).astype(BF16)) for p in pairs]
        yield
        u = [u[p] + res[p][:, :LANES] for p in pairs]
        mk = [res[p][:, LANES:] for p in pairs]
    u = [u[p] + _dot(mk[p].astype(BF16), stack(u[p]).astype(BF16)) for p in pairs]
    yield

    rbk = [jnp.concatenate([jnp.where(incl, pm[p][CHUNK:, :LANES], 0.0),
                            jnp.where(incl, pm[p][CHUNK:, LANES:], 0.0)], axis=1).astype(BF16) for p in pairs]
    y = [a_s[p][CHUNK:] + _dot(rbk[p], jnp.concatenate([stack(u[p]).astype(BF16), v_st[p]], axis=0))
         for p in pairs]
    uvt = [jnp.concatenate([u[p], vp[p]], axis=0).T.astype(BF16) for p in pairs]
    bkt = [jnp.concatenate([bt[p], kt[p]], axis=0).astype(BF16) for p in pairs]
    for p in pairs:
        st_ref[p] = (st[p] + jnp.where(blockdiag, _dot(uvt[p], bkt[p]), 0.0)) * decay[p]
    yield

    y = jnp.concatenate(y, axis=1)
    mean = gsum(y) * (1.0 / C_DH)
    yield
    yc = y - mean
    var = gsum(yc * yc) * (1.0 / C_DH)
    yield
    yn = yc * lax.rsqrt(var + C_LN_EPS) * lng_ref[...] + lnb_ref[...]
    o_ref[rows, cols] = ((yn + bonus_s * v) * _silu(z_ref[rows, :])).astype(o_ref.dtype)


N_A_IN, N_B_IN, N_C_IN = 8, 11, 20
N_A_SCR, N_B_SCR, N_C_SCR = 1, 4, 5


def _mixers_kernel(*refs, chunks):
    refs = list(refs)

    def take(n):
        out = refs[:n]
        del refs[:n]
        return out

    a_in, b_in, c_in = take(N_A_IN), take(N_B_IN), take(N_C_IN)
    (y_ref,) = take(1)
    a_scr, b_scr, c_scr = take(N_A_SCR), take(N_B_SCR), take(N_C_SCR)

    @pl.when(pl.program_id(1) == 0)
    def _():
        for ref in a_scr + b_scr + c_scr:
            ref[...] = jnp.zeros_like(ref)

    def chunk(c, carry):
        rows = pl.ds(pl.multiple_of(c * CHUNK, CHUNK), CHUNK)
        w = BRANCH_WIDTH
        live = [_rwkv_chunk(rows, slice(2 * w, 3 * w), *c_in, y_ref, *c_scr),
                _hgrn2_chunk(rows, slice(0, w), *a_in, y_ref, *a_scr),
                _mlstm_chunk(rows, slice(w, 2 * w), *b_in, y_ref, *b_scr)]
        stages = dict(zip(map(id, live), (2, 1, 1)))
        while live:
            for gen in list(live):
                for _ in range(stages[id(gen)]):
                    if next(gen, "done") == "done":
                        live.remove(gen)
                        break
        return carry

    lax.fori_loop(0, chunks, chunk, 0)


def _mixers(proj, a_par, b_par, c_par, bsz, t_pad, tb):
    m = proj.shape[0]
    nt = t_pad // tb
    w = BRANCH_WIDTH
    qw = B_HEADS * B_DQK

    def col(off, width):
        return pl.BlockSpec((tb, width), lambda b, t: (b * nt + t, off // width))

    def whole(a):
        return pl.BlockSpec(a.shape, lambda b, t: (0,) * a.ndim)

    a_cols = [col(A_Q, w), col(A_F, w), col(A_I, w), col(A_Z, w)]
    b_cols = [col(B_Q, qw), col(B_K, qw), col(B_V, w), col(B_O, w), col(B_Z, w), col(B_G, LANES)]
    c_cols = [col(C_R, w), col(C_K, w), col(C_V, w), col(C_Z, w), col(C_LR, LANES)]
    conv_w = b_par[0]
    b_specs = [pl.BlockSpec((B_CONV, qw), lambda b, t: (0, 0)),
               pl.BlockSpec((B_CONV, qw), lambda b, t: (0, 1))] + [whole(a) for a in b_par[1:]]
    operands = ([proj] * len(a_cols) + list(a_par) + [proj] * len(b_cols) + [conv_w] + list(b_par)
                + [proj] * len(c_cols) + list(c_par))
    in_specs = (a_cols + [whole(a) for a in a_par] + b_cols + b_specs + c_cols + [whole(a) for a in c_par])
    assert len(a_cols) + len(a_par) == N_A_IN and len(b_cols) + len(b_specs) == N_B_IN
    assert len(c_cols) + len(c_par) == N_C_IN
    return pl.pallas_call(
        functools.partial(_mixers_kernel, chunks=tb // CHUNK),
        grid=(bsz, nt),
        in_specs=in_specs,
        out_specs=pl.BlockSpec((tb, N_BRANCH * w), lambda b, t: (b * nt + t, 0)),
        out_shape=jax.ShapeDtypeStruct((m, N_BRANCH * w), BF16),
        scratch_shapes=[pltpu.VMEM((A_HEADS, LANES, LANES), F32),
                        pltpu.VMEM((B_HEADS, B_DQK, B_DV + LANES), F32),
                        pltpu.VMEM((B_HEADS, SUBLANES, LANES), F32),
                        pltpu.VMEM((SUBLANES, qw), F32),
                        pltpu.VMEM((SUBLANES, qw), F32),
                        pltpu.VMEM((w // LANES, LANES, LANES), F32),
                        pltpu.VMEM((SUBLANES, w), F32),
                        pltpu.VMEM((SUBLANES, w), F32),
                        pltpu.VMEM((SUBLANES, w), F32),
                        pltpu.VMEM((SUBLANES, LANES), F32)],
        compiler_params=_params(("arbitrary", "arbitrary")),
        name="mixers",
    )(*operands)


def _post_kernel(y_ref, g_ref, wbr_ref, wout_ref, h_ref, gn_ref,
                 o_ref, hg_ref, rs_ref, mg_ref, ss_ref, *, nsub, tn):
    j = pl.program_id(2)

    @pl.when(j == 0)
    def _():
        ss_ref[...] = jnp.zeros_like(ss_ref)

    @pl.when(j < nsub)
    def _():
        w = BRANCH_WIDTH
        acc = _sigmoid(g_ref[:, :tn]) * _dot(y_ref[:, :w], wbr_ref[0])
        for b in range(1, N_BRANCH):
            acc = acc + _sigmoid(g_ref[:, b * tn:(b + 1) * tn]) * _dot(y_ref[:, b * w:(b + 1) * w], wbr_ref[b])
        mg_ref[j] = acc.astype(BF16)

    @pl.when(j >= nsub)
    def _():
        new = h_ref[...]
        for kk in range(nsub):
            new = new + _dot(mg_ref[kk], wout_ref[kk * tn:(kk + 1) * tn, :])
        row = lax.broadcasted_iota(jnp.int32, new.shape, 0)
        keep = jnp.logical_or(pl.program_id(1) > 0, row >= PAD_FRONT)
        new = jnp.where(keep, new, 0.0)
        o_ref[...] = new
        hg_ref[...] = (new * gn_ref[...]).astype(hg_ref.dtype)
        ss_ref[...] += jnp.sum(new * new, axis=-1, keepdims=True)

    @pl.when(j == 2 * nsub - 1)
    def _():
        rs_ref[...] = lax.rsqrt(ss_ref[...] * (1.0 / (nsub * tn)) + NORM_EPS)


def _post(y, proj, w_br, w_out, h, g_next, layer, bsz, t_pad, tm, tn):
    m, d = h.shape
    nt = t_pad // tm
    nsub = d // tn
    assert tn == GATE_TILE and d == D_GATE
    y_spec = pl.BlockSpec((tm, N_BRANCH * BRANCH_WIDTH), lambda b, t, j: (b * nt + t, 0))
    gate_spec = pl.BlockSpec((tm, N_BRANCH * tn),
                             lambda b, t, j: (b * nt + t, G_BASE // (N_BRANCH * tn) + jnp.minimum(j, nsub - 1)))

    def out_col(j):
        return jnp.maximum(j - nsub, 0)

    return pl.pallas_call(
        functools.partial(_post_kernel, nsub=nsub, tn=tn),
        grid=(bsz, nt, 2 * nsub),
        in_specs=[y_spec, gate_spec,
                  pl.BlockSpec((None, 3, BRANCH_WIDTH, tn), lambda b, t, j: (layer, 0, 0, jnp.minimum(j, nsub - 1))),
                  pl.BlockSpec((None, d, tn), lambda b, t, j: (layer, 0, out_col(j))),
                  pl.BlockSpec((tm, tn), lambda b, t, j: (b * nt + t, out_col(j))),
                  pl.BlockSpec((1, tn), lambda b, t, j: (0, out_col(j)))],
        out_specs=[pl.BlockSpec((tm, tn), lambda b, t, j: (b * nt + t, out_col(j))),
                   pl.BlockSpec((tm, tn), lambda b, t, j: (b * nt + t, out_col(j))),
                   pl.BlockSpec((tm, LANES), lambda b, t, j: (b * nt + t, 0))],
        out_shape=[jax.ShapeDtypeStruct((m, d), F32), jax.ShapeDtypeStruct((m, d), BF16),
                   jax.ShapeDtypeStruct((m, LANES), F32)],
        scratch_shapes=[pltpu.VMEM((nsub, tm, tn), BF16), pltpu.VMEM((tm, LANES), F32)],
        input_output_aliases={4: 0},
        compiler_params=_params(("arbitrary", "arbitrary", "arbitrary")),
        name="post",
    )(y, proj, w_br, w_out, h, g_next)


SRC_B_G = B_O + BRANCH_WIDTH
SRC_C_LR = SRC_B_G + 2 * B_HEADS + 4 * BRANCH_WIDTH
N_IN = SRC_C_LR + LANES + BRANCH_WIDTH + N_BRANCH * D_GATE
RG_TILE = 512
RG_SRC_TILES = -(-N_IN // RG_TILE)
RG_ROWS = 256
RG_RUNS = ((0, SRC_B_G // RG_TILE, 0),
           (B_Z // RG_TILE, C_Z // RG_TILE, SRC_B_G + 2 * B_HEADS - B_Z),
           (C_Z // RG_TILE, B_G // RG_TILE, SRC_C_LR + LANES - C_Z))


def _regroup_kernel(x_ref, o_ref, t2_ref, g_ref, c_ref):
    s = pl.program_id(1)
    d = t2_ref.shape[0]

    @pl.when(s >= 1)
    def _():
        t2_ref[:, :RG_TILE] = t2_ref[:, RG_TILE:]

    @pl.when(s < RG_SRC_TILES)
    def _():
        for r in range(RG_TILE // LANES):
            t2_ref[:, RG_TILE + r * LANES:RG_TILE + (r + 1) * LANES] = x_ref[r * LANES:(r + 1) * LANES, :].T

    @pl.when(s == SRC_B_G // RG_TILE)
    def _():
        g_ref[...] = t2_ref[:, RG_TILE:RG_TILE + LANES]

    @pl.when(s == SRC_C_LR // RG_TILE)
    def _():
        c_ref[...] = t2_ref[:, RG_TILE:RG_TILE + 2 * LANES]

    def emit(shift):
        base = shift // LANES * LANES
        for r in range(0, d, RG_ROWS):
            if shift == base:
                tile = t2_ref[r:r + RG_ROWS, base:base + RG_TILE]
            else:
                tile = t2_ref[r:r + RG_ROWS, base:base + RG_TILE + LANES][:, shift - base:shift - base + RG_TILE]
            o_ref[r:r + RG_ROWS, :] = tile.astype(BF16)

    t = s - 1
    for first, end, shift in RG_RUNS:
        pl.when(jnp.logical_and(t >= first, t < end))(functools.partial(emit, shift))

    @pl.when(t == B_G // RG_TILE)
    def _():
        lane = lax.broadcasted_iota(jnp.int32, (RG_ROWS, LANES), 1)
        off = SRC_C_LR % RG_TILE
        for r in range(0, d, RG_ROWS):
            rows = slice(r, r + RG_ROWS)
            o_ref[rows, :LANES] = jnp.where(lane < 2 * B_HEADS, g_ref[rows, :], 0.0).astype(BF16)
            o_ref[rows, LANES:2 * LANES] = c_ref[rows, :][:, off:off + LANES].astype(BF16)
            o_ref[rows, 2 * LANES:] = jnp.zeros((RG_ROWS, RG_TILE - 2 * LANES), BF16)


def _gate_tile_position(t):
    first = G_BASE // RG_TILE
    per_branch = D_GATE // RG_TILE
    g = t - first
    moved = first + N_BRANCH * (g % per_branch) + g // per_branch
    return jnp.where(jnp.logical_and(g >= 0, g < N_BRANCH * per_branch), moved, t)


def _regroup_w_in(w_in):
    depth, d, n_in = w_in.shape
    assert n_in == N_IN and d % RG_ROWS == 0 and N_PROJ == (B_G // RG_TILE + 1) * RG_TILE
    assert C_LR == B_G + LANES and SRC_C_LR % RG_TILE + LANES <= 2 * LANES and RG_TILE == GATE_TILE
    return pl.pallas_call(
        _regroup_kernel,
        grid=(depth, RG_SRC_TILES + 1),
        in_specs=[pl.BlockSpec((None, RG_TILE, d), lambda l, s: (l, jnp.minimum(s, RG_SRC_TILES - 1), 0))],
        out_specs=pl.BlockSpec((None, d, RG_TILE), lambda l, s: (l, 0, _gate_tile_position(jnp.maximum(s - 1, 0)))),
        out_shape=jax.ShapeDtypeStruct((depth, d, N_PROJ), BF16),
        scratch_shapes=[pltpu.VMEM((d, 2 * RG_TILE), F32), pltpu.VMEM((d, LANES), F32),
                        pltpu.VMEM((d, 2 * LANES), F32)],
        compiler_params=_params(("arbitrary", "arbitrary")),
        name="regroup_w_in",
    )(jnp.swapaxes(w_in, 1, 2))


def _rwkv_up_weights(w_up, a_up):
    top = jnp.concatenate([w_up, jnp.zeros_like(w_up)], axis=-1)
    bot = jnp.concatenate([jnp.zeros_like(a_up), a_up], axis=-1)
    w2 = jnp.concatenate([top, bot], axis=1)
    hi = w2.astype(BF16)
    lo = (w2 - hi.astype(F32)).astype(BF16)
    return hi, lo


def kernel(x, meta_tokens, norm_g, w_in, hgrn_lb_logits, hgrn_norm_g, mlstm_conv, mlstm_ig_b, mlstm_fg_b,
           mlstm_norm_g, rwkv_mu, rwkv_w0, rwkv_w_up, rwkv_a0, rwkv_a_up, rwkv_k_k, rwkv_k_a, rwkv_r_k,
           rwkv_ln_g, rwkv_ln_b, w_br, w_out, final_norm_g):
    bsz, seq, d = x.shape
    depth = w_in.shape[0]
    t_pad = PAD_FRONT + N_META + seq
    m = bsz * t_pad
    assert seq % (2 * CHUNK) == 0 and t_pad % CHUNK == 0

    tb = _divisor_tile(t_pad, 320, CHUNK)
    tm_in = _divisor_tile(m, 3328, LANES)
    tm_row = _divisor_tile(m, 640, LANES)
    tm_post = _divisor_tile(t_pad, 1040, 2 * SUBLANES)
    assert tm_post >= PAD_FRONT

    w_in_p = _regroup_w_in(w_in)
    w_br_b = w_br.astype(BF16)
    w_out_b = w_out.astype(BF16)
    p = jax.nn.softmax(hgrn_lb_logits.astype(F32), axis=0)
    lower_bounds = jnp.cumsum(p, axis=0) - p[0]
    gate_b = jnp.concatenate([mlstm_ig_b, mlstm_fg_b, jnp.zeros((depth, LANES - 2 * B_HEADS), F32)], axis=-1)
    w2h, w2l = _rwkv_up_weights(rwkv_w_up, rwkv_a_up)
    w = BRANCH_WIDTH
    mu_r, mu_k, mu_v, mu_lr = rwkv_mu[:, :w], rwkv_mu[:, w:2 * w], rwkv_mu[:, 2 * w:3 * w], rwkv_mu[:, 3 * w:]
    tri = jnp.asarray(np.tril(np.ones((CHUNK, CHUNK), np.float32)), BF16)
    levels = jnp.asarray(_hgrn2_levels())
    lane_head = np.arange(LANES) // C_DH
    ones_bd = jnp.asarray((lane_head[:, None] == lane_head[None, :]).astype(np.float32), BF16)

    meta = jnp.broadcast_to(meta_tokens[None].astype(F32), (bsz, N_META, d))
    h = jnp.concatenate([jnp.zeros((bsz, PAD_FRONT, d), F32), meta, x.astype(F32)], axis=1).reshape(m, d)

    hg = _rmsnorm(h, norm_g[0][None, :], BF16, tm_row)
    rs = jnp.ones((m, LANES), F32)
    for l in range(depth):
        row = lambda a: a[l][None, :]
        proj = _inproj(hg, rs, w_in_p, l, tm_in, 512)
        a_par = (row(lower_bounds), row(hgrn_norm_g), tri, levels)
        b_par = (mlstm_conv[l], row(gate_b), row(mlstm_norm_g), tri)
        c_par = (row(mu_r), row(mu_k), row(mu_v), row(mu_lr), row(rwkv_w0), row(rwkv_a0), row(rwkv_k_k),
                 row(rwkv_k_a), row(rwkv_r_k), row(rwkv_ln_g), row(rwkv_ln_b), w2h[l], w2l[l], tri, ones_bd)
        y = _mixers(proj, a_par, b_par, c_par, bsz, t_pad, tb)
        g_next = norm_g[min(l + 1, depth - 1)][None, :]
        h, hg, rs = _post(y, proj, w_br_b, w_out_b, h, g_next, l, bsz, t_pad, tm_post, GATE_TILE)

    return _final_norm(h, final_norm_g[None, :], bsz, t_pad, seq)
```

```python
import functools

import numpy as np
import jax
import jax.numpy as jnp
from jax import lax
from jax.experimental import pallas as pl
from jax.experimental.pallas import tpu as pltpu

F32 = jnp.float32
BF16 = jnp.bfloat16

N_META = 16
CHUNK = 64
NORM_EPS = 1e-6
NEG_BIG = -1e30
F_FLOOR = 1e-12
BRANCH_WIDTH = 1024
A_HEADS, A_DK = 8, 128
B_HEADS, B_DQK, B_DV, B_CONV = 4, 128, 256, 4
C_HEADS, C_DH, C_RANK = 16, 64, 64
C_LN_EPS = 64e-5
C_DECAY_MAX = float(np.exp(-0.5))

LANES = 128
SUBLANES = 8
PAD_FRONT = 2 * CHUNK - N_META
VMEM_LIMIT = 56 * 1024 * 1024

A_Q, A_F, A_I, A_Z = 0, 1024, 2048, 3072
B_Q, B_K, B_V, B_O, B_Z = 4096, 4608, 5120, 6144, 7168
C_R, C_K, C_V, C_Z = 8192, 9216, 10240, 11264
G_BASE = 12288
N_BRANCH = 3
D_GATE = 2048
GATE_TILE = 512
B_G = 18432
C_LR = 18560
N_PROJ = 18944


def _divisor_tile(n, target, mult):
    best = None
    for d in range(mult, min(n, target) + 1, mult):
        if n % d == 0:
            best = d
    if best is None:
        raise ValueError(f"no tile for {n} (multiple of {mult}, <= {target})")
    return best


def _params(sem):
    return pltpu.CompilerParams(dimension_semantics=sem, vmem_limit_bytes=VMEM_LIMIT)


def _sigmoid(x):
    return 0.5 * jnp.tanh(0.5 * x) + 0.5


def _silu(x):
    return x * _sigmoid(x)


def _dot(a, b):
    return jnp.dot(a, b, preferred_element_type=F32)


def _dot_nt(a, b):
    return lax.dot_general(a, b, (((1,), (1,)), ((), ())), preferred_element_type=F32)


def _split(x, pieces):
    out = []
    for _ in range(pieces - 1):
        part = x.astype(BF16)
        out.append(part)
        x = x - part.astype(F32)
    return out + [x.astype(BF16)]


def _cumsum_rows(tri, x, pieces=3):
    w = x.shape[1]
    r = _dot(tri, jnp.concatenate(_split(x, pieces), axis=1))
    acc = r[:, (pieces - 1) * w:]
    for i in range(pieces - 2, -1, -1):
        acc = acc + r[:, i * w:(i + 1) * w]
    return acc


def _row_bcast(x, r, n):
    return jnp.broadcast_to(x[r:r + 1, :], (n, x.shape[1]))


def _shift_rows(x, tail, j):
    rolled = pltpu.roll(x, j, 0)
    tail_r = pltpu.roll(tail, j, 0)
    row = lax.broadcasted_iota(jnp.int32, tail.shape, 0)
    first = jnp.where(row < j, tail_r, rolled[:SUBLANES])
    return jnp.concatenate([first, rolled[SUBLANES:]], axis=0)


def _tiles(x, w=LANES):
    return [x[:, i * w:(i + 1) * w] for i in range(x.shape[1] // w)]


def _rmsnorm_kernel(x_ref, g_ref, o_ref):
    x = x_ref[...]
    ms = jnp.mean(x * x, axis=-1, keepdims=True)
    o_ref[...] = (x * lax.rsqrt(ms + NORM_EPS) * g_ref[...]).astype(o_ref.dtype)


def _rmsnorm(h, g, out_dtype, tm):
    m, d = h.shape
    return pl.pallas_call(
        _rmsnorm_kernel,
        grid=(m // tm,),
        in_specs=[pl.BlockSpec((tm, d), lambda i: (i, 0)), pl.BlockSpec((1, d), lambda i: (0, 0))],
        out_specs=pl.BlockSpec((tm, d), lambda i: (i, 0)),
        out_shape=jax.ShapeDtypeStruct((m, d), out_dtype),
        compiler_params=_params(("arbitrary",)),
        name="rmsnorm",
    )(h, g)


def _final_norm_kernel(x_ref, g_ref, o_ref):
    x = x_ref[...]
    ms = jnp.mean(x * x, axis=-1, keepdims=True)
    o_ref[0] = x * lax.rsqrt(ms + NORM_EPS) * g_ref[...]


def _final_norm(h, g, bsz, t_pad, seq):
    d = h.shape[1]
    tm = CHUNK * 2
    per_b = t_pad // tm
    return pl.pallas_call(
        _final_norm_kernel,
        grid=(bsz, seq // tm),
        in_specs=[pl.BlockSpec((tm, d), lambda b, i: (b * per_b + i + 1, 0)),
                  pl.BlockSpec((1, d), lambda b, i: (0, 0))],
        out_specs=pl.BlockSpec((1, tm, d), lambda b, i: (b, i, 0)),
        out_shape=jax.ShapeDtypeStruct((bsz, seq, d), F32),
        compiler_params=_params(("arbitrary", "arbitrary")),
        name="final_norm",
    )(h, g)


def _inproj_kernel(x_ref, rs_ref, w_ref, o_ref):
    o_ref[...] = _dot(x_ref[...], w_ref[...]) * rs_ref[:, :1]


def _inproj(hg, rs, w, layer, tm, tn):
    m, d = hg.shape
    n = w.shape[2]
    return pl.pallas_call(
        _inproj_kernel,
        grid=(m // tm, n // tn),
        in_specs=[pl.BlockSpec((tm, d), lambda i, j: (i, 0)),
                  pl.BlockSpec((tm, LANES), lambda i, j: (i, 0)),
                  pl.BlockSpec((None, d, tn), lambda i, j: (layer, 0, j))],
        out_specs=pl.BlockSpec((tm, tn), lambda i, j: (i, j)),
        out_shape=jax.ShapeDtypeStruct((m, n), F32),
        compiler_params=_params(("arbitrary", "arbitrary")),
        name="inproj",
    )(hg, rs, w)


def _hgrn2_levels():
    t = np.arange(CHUNK)[:, None]
    s = np.arange(CHUNK)[None, :]
    lv = np.zeros((CHUNK, CHUNK), np.int32)
    lv[(t // 4 == s // 4) & (s <= t)] = 1
    for lid, sz in ((2, 4), (3, 8), (4, 16), (5, 32)):
        lv[(t // (2 * sz) == s // (2 * sz)) & (t % (2 * sz) >= sz) & (s % (2 * sz) < sz)] = lid
    return lv


def _hgrn2_chunk(rows, cols, q_ref, f_ref, i_ref, z_ref, lb_ref, g_ref, tri_ref, lv_ref, o_ref, st_ref):
    heads = range(A_HEADS)
    lv = lv_ref[...]
    sub = lax.broadcasted_iota(jnp.int32, (SUBLANES, BRANCH_WIDTH), 0)
    lb = lb_ref[...]
    fr = f_ref[rows, :]
    v = i_ref[rows, :]
    qs = _silu(q_ref[rows, :]) * (A_DK ** -0.5)
    sg = _sigmoid(fr)
    k = (1.0 - lb) * (1.0 - sg)
    lf = jnp.log(jnp.maximum(lb + (1.0 - lb) * sg, F_FLOOR))
    cg = _cumsum_rows(tri_ref[...], lf)
    yield
    cl = cg[CHUNK - 1:CHUNK, :]
    st = [st_ref[j] for j in heads]
    qg = _tiles((qs * jnp.exp(cg)).astype(BF16))
    o = [_dot_nt(qg[j], st[j].astype(BF16)) for j in heads]
    yield

    ref = jnp.concatenate(
        [jnp.where(sub < 4, _row_bcast(cg, 8 * g, SUBLANES), _row_bcast(cg, 8 * g + 4, SUBLANES))
         for g in range(CHUNK // SUBLANES)], axis=0)
    qt = _tiles((qs * jnp.exp(cg - ref)).astype(BF16))
    kt = _tiles((k * jnp.exp(ref - cg)).astype(BF16))
    att = [jnp.where(lv == 1, _dot_nt(qt[j], kt[j]), 0.0) for j in heads]
    yield
    row_i = lax.broadcasted_iota(jnp.int32, (CHUNK, BRANCH_WIDTH), 0)
    for lid, sz in ((2, 4), (3, 8), (4, 16), (5, 32)):
        ref = jnp.concatenate(
            [_row_bcast(cg, g * 2 * sz + sz - 1, 2 * sz) for g in range(CHUNK // (2 * sz))], axis=0)
        later = (row_i & sz) != 0
        x = _tiles((jnp.where(later, qs, k) * jnp.exp(-jnp.abs(cg - ref))).astype(BF16))
        att = [jnp.where(lv == lid, _dot_nt(x[j], x[j]), att[j]) for j in heads]
        yield

    vb = _tiles(v.astype(BF16))
    o = [o[j] + _dot(att[j].astype(BF16), vb[j]) for j in heads]
    yield
    kd = _tiles((k * jnp.exp(cl - cg)).astype(BF16))
    vt = [t.T.astype(BF16) for t in _tiles(v)]
    decay = _tiles(jnp.exp(cl))
    for j in heads:
        st_ref[j] = st[j] * decay[j] + _dot(vt[j], kd[j])
    yield
    on = [o[j] * lax.rsqrt(jnp.mean(o[j] * o[j], axis=-1, keepdims=True) + NORM_EPS) for j in heads]
    y = jnp.concatenate(on, axis=1) * g_ref[...] * _silu(z_ref[rows, :])
    o_ref[rows, cols] = y.astype(o_ref.dtype)


def _mlstm_chunk(rows, cols, q_ref, k_ref, v_ref, og_ref, z_ref, gt_ref, cwq_ref, cwk_ref, gb_ref, ng_ref, tri_ref,
                 o_ref, c_ref, m_ref, tq_ref, tk_ref):
    heads = range(B_HEADS)

    def conv(x, tail, w):
        acc = x * w[B_CONV - 1:B_CONV, :]
        for j in range(1, B_CONV):
            acc = acc + _shift_rows(x, tail, j) * w[B_CONV - 1 - j:B_CONV - j, :]
        return _silu(acc)

    t_i = lax.broadcasted_iota(jnp.int32, (CHUNK, CHUNK), 0)
    s_i = lax.broadcasted_iota(jnp.int32, (CHUNK, CHUNK), 1)
    causal = s_i <= t_i
    lane = lax.broadcasted_iota(jnp.int32, (CHUNK, LANES), 1)
    sub = lax.broadcasted_iota(jnp.int32, (LANES, CHUNK), 0)

    gates = gt_ref[rows, :] + gb_ref[...]
    lf = jnp.minimum(gates, 0.0) - jnp.log(1.0 + jnp.exp(-jnp.abs(gates)))
    bcum = _cumsum_rows(tri_ref[...], lf)
    yield
    qraw = q_ref[rows, :]
    kraw = k_ref[rows, :]
    qc = _tiles((conv(qraw, tq_ref[...], cwq_ref[...]) * (B_DQK ** -0.5)).astype(BF16), B_DQK)
    kc32 = _tiles(conv(kraw, tk_ref[...], cwk_ref[...]), B_DQK)
    kc = [t.astype(BF16) for t in kc32]
    tq_ref[...] = qraw[CHUNK - SUBLANES:, :]
    tk_ref[...] = kraw[CHUNK - SUBLANES:, :]
    c_aug = [c_ref[j] for j in heads]
    qk = [_dot_nt(qc[j], kc[j]) for j in heads]
    qcs = [_dot(qc[j], c_aug[j].astype(BF16)) for j in heads]
    yield

    gates_t = gates.T
    bcum_t = bcum.T
    ig_col = [jnp.sum(jnp.where(lane == j, gates, 0.0), axis=1, keepdims=True) for j in heads]
    b_col = [jnp.sum(jnp.where(lane == B_HEADS + j, bcum, 0.0), axis=1, keepdims=True) for j in heads]
    ig_row = [jnp.sum(jnp.where(sub == j, gates_t, 0.0), axis=0, keepdims=True) for j in heads]
    b_row = [jnp.sum(jnp.where(sub == B_HEADS + j, bcum_t, 0.0), axis=0, keepdims=True) for j in heads]
    yield

    ones = jnp.ones((CHUNK, LANES), BF16)
    v_aug = [jnp.concatenate([t.astype(BF16), ones], axis=1) for t in _tiles(v_ref[rows, :], B_DV)]
    m_prev = [m_ref[j][0:1, 0:1] for j in heads]
    log_w = [jnp.where(causal, b_col[j] - b_row[j] + ig_row[j], NEG_BIG) for j in heads]
    log_inter = [b_col[j] + m_prev[j] for j in heads]
    m_t = [jnp.maximum(log_inter[j], jnp.max(log_w[j], axis=-1, keepdims=True)) for j in heads]
    scores = [(qk[j] * jnp.exp(log_w[j] - m_t[j])).astype(BF16) for j in heads]
    numden = [_dot(scores[j], v_aug[j]) + jnp.exp(log_inter[j] - m_t[j]) * qcs[j] for j in heads]
    yield

    b_end = [b_col[j][CHUNK - 1:CHUNK, :] for j in heads]
    log_s = [b_end[j] - b_col[j] + ig_col[j] for j in heads]
    m_new = [jnp.maximum(b_end[j] + m_prev[j], jnp.max(log_s[j], axis=0, keepdims=True)) for j in heads]
    kw = [(kc32[j] * jnp.exp(log_s[j] - m_new[j])).T.astype(BF16) for j in heads]
    for j in heads:
        c_ref[j] = jnp.exp(b_end[j] + m_prev[j] - m_new[j]) * c_aug[j] + _dot(kw[j], v_aug[j])
        m_ref[j] = jnp.broadcast_to(m_new[j], (SUBLANES, LANES))
    yield

    hid = [numden[j][:, :B_DV] / jnp.maximum(jnp.abs(numden[j][:, B_DV:B_DV + 1]), jnp.exp(-m_t[j]))
           for j in heads]
    hc = [hid[j] - jnp.mean(hid[j], axis=-1, keepdims=True) for j in heads]
    hn = [hc[j] * lax.rsqrt(jnp.mean(hc[j] * hc[j], axis=-1, keepdims=True) + NORM_EPS) for j in heads]
    y = jnp.concatenate(hn, axis=1) * ng_ref[...] * _sigmoid(og_ref[rows, :]) * _silu(z_ref[rows, :])
    o_ref[rows, cols] = y.astype(o_ref.dtype)


def _rwkv_chunk(rows, cols, r_ref, k_ref, v_ref, z_ref, lr_ref, mur_ref, muk_ref, muv_ref, mulr_ref, w0_ref, a0_ref,
                kk_ref, ka_ref, rk_ref, lng_ref, lnb_ref, w2h_ref, w2l_ref, tri_ref, ones_ref,
                o_ref, st_ref, tr_ref, tk_ref, tv_ref, tlr_ref):
    npairs = BRANCH_WIDTH // LANES
    pairs = range(npairs)

    def lerp(x_ref, tail_ref, mu_ref):
        x = x_ref[rows, :]
        out = x + (_shift_rows(x, tail_ref[...], 1) - x) * mu_ref[...]
        tail_ref[...] = x[CHUNK - SUBLANES:, :]
        return out

    tri = tri_ref[...]
    ones_bd = ones_ref[...]
    lane = lax.broadcasted_iota(jnp.int32, (CHUNK, LANES), 1)
    t_i = lax.broadcasted_iota(jnp.int32, (CHUNK, LANES), 0)
    s_i = lane % C_DH
    head0 = lane < C_DH
    strict = s_i < t_i
    incl = s_i <= t_i
    bd_r = lax.broadcasted_iota(jnp.int32, (LANES, LANES), 0) // C_DH
    bd_c = lax.broadcasted_iota(jnp.int32, (LANES, LANES), 1) // C_DH
    blockdiag = bd_r == bd_c

    def stack(x):
        return jnp.concatenate([jnp.where(head0, x, 0.0), jnp.where(head0, 0.0, x)], axis=0)

    def gsum(x):
        res = _dot(jnp.concatenate(_tiles(x.astype(BF16)), axis=0), ones_bd)
        return jnp.concatenate([res[p * CHUNK:(p + 1) * CHUNK] for p in pairs], axis=1)

    lr = lerp(lr_ref, tlr_ref, mulr_ref)
    lr = jnp.where(head0, jnp.tanh(lr), lr)
    lr_hi = lr.astype(BF16)
    lr_lo = (lr - lr_hi.astype(F32)).astype(BF16)
    up2 = _dot(jnp.concatenate([lr_hi, lr_lo], axis=0), w2h_ref[...])
    up_lo = _dot(lr_hi, w2l_ref[...])
    k0 = lerp(k_ref, tk_ref, muk_ref)
    kk = k0 * kk_ref[...]
    kk_ss = gsum(kk * kk)
    yield
    up = up2[:CHUNK] + up2[CHUNK:] + up_lo
    lw = -C_DECAY_MAX * _sigmoid(w0_ref[...] + up[:, :BRANCH_WIDTH])
    cw = _cumsum_rows(tri, lw, pieces=2)
    yield
    r = lerp(r_ref, tr_ref, mur_ref)
    v = lerp(v_ref, tv_ref, muv_ref)
    a = _sigmoid(a0_ref[...] + up[:, BRANCH_WIDTH:])
    kk = kk / jnp.maximum(jnp.sqrt(kk_ss), 1e-12)
    k = k0 * (1.0 + (a - 1.0) * ka_ref[...])
    b = kk * a
    bonus_s = gsum(r * k * rk_ref[...])
    e_c = jnp.exp(cw)
    e_n = jnp.exp(-cw)
    at = _tiles(-kk * jnp.exp(cw - lw))
    rt = _tiles(r * e_c)
    bt = _tiles(b * e_n)
    kt = _tiles(k * e_n)
    vp = _tiles(v)
    decay = _tiles(e_c[CHUNK - 1:CHUNK, :])
    st = [st_ref[p] for p in pairs]

    ar = [jnp.concatenate([at[p], rt[p]], axis=0).astype(BF16) for p in pairs]
    bk = [jnp.concatenate([stack(bt[p]), stack(kt[p])], axis=0).astype(BF16) for p in pairs]
    pm = [_dot_nt(ar[p], bk[p]) for p in pairs]
    a_s = [_dot_nt(ar[p], st[p].astype(BF16)) for p in pairs]
    yield
    v_st = [stack(vp[p]).astype(BF16) for p in pairs]
    l_ak = [jnp.where(strict, pm[p][:CHUNK, LANES:], 0.0).astype(BF16) for p in pairs]
    u = [a_s[p][:CHUNK] + _dot(l_ak[p], v_st[p]) for p in pairs]
    yield
    mk = [jnp.where(strict, pm[p][:CHUNK, :LANES], 0.0) for p in pairs]
    steps = CHUNK.bit_length() - 1
    for it in range(steps - 1):
        res = [_dot(mk[p].astype(BF16),
                    jnp.concatenate([stack(u[p]), stack(mk[p])], axis=1).astype(BF16)) for p in pairs]
        yield
        u = [u[p] + res[p][:, :LANES] for p in pairs]
        mk = [res[p][:, LANES:] for p in pairs]
    u = [u[p] + _dot(mk[p].astype(BF16), stack(u[p]).astype(BF16)) for p in pairs]
    yield

    rbk = [jnp.concatenate([jnp.where(incl, pm[p][CHUNK:, :LANES], 0.0),
                            jnp.where(incl, pm[p][CHUNK:, LANES:], 0.0)], axis=1).astype(BF16) for p in pairs]
    y = [a_s[p][CHUNK:] + _dot(rbk[p], jnp.concatenate([stack(u[p]).astype(BF16), v_st[p]], axis=0))
         for p in pairs]
    uvt = [jnp.concatenate([u[p], vp[p]], axis=0).T.astype(BF16) for p in pairs]
    bkt = [jnp.concatenate([bt[p], kt[p]], axis=0).astype(BF16) for p in pairs]
    for p in pairs:
        st_ref[p] = (st[p] + jnp.where(blockdiag, _dot(uvt[p], bkt[p]), 0.0)) * decay[p]
    yield

    y = jnp.concatenate(y, axis=1)
    mean = gsum(y) * (1.0 / C_DH)
    yield
    yc = y - mean
    var = gsum(yc * yc) * (1.0 / C_DH)
    yield
    yn = yc * lax.rsqrt(var + C_LN_EPS) * lng_ref[...] + lnb_ref[...]
    o_ref[rows, cols] = ((yn + bonus_s * v) * _silu(z_ref[rows, :])).astype(o_ref.dtype)


N_A_IN, N_B_IN, N_C_IN = 8, 11, 20
N_A_SCR, N_B_SCR, N_C_SCR = 1, 4, 5


def _mixers_kernel(*refs, chunks):
    refs = list(refs)

    def take(n):
        out = refs[:n]
        del refs[:n]
        return out

    a_in, b_in, c_in = take(N_A_IN), take(N_B_IN), take(N_C_IN)
    (y_ref,) = take(1)
    a_scr, b_scr, c_scr = take(N_A_SCR), take(N_B_SCR), take(N_C_SCR)

    @pl.when(pl.program_id(1) == 0)
    def _():
        for ref in a_scr + b_scr + c_scr:
            ref[...] = jnp.zeros_like(ref)

    def chunk(c, carry):
        rows = pl.ds(pl.multiple_of(c * CHUNK, CHUNK), CHUNK)
        w = BRANCH_WIDTH
        live = [_rwkv_chunk(rows, slice(2 * w, 3 * w), *c_in, y_ref, *c_scr),
                _hgrn2_chunk(rows, slice(0, w), *a_in, y_ref, *a_scr),
                _mlstm_chunk(rows, slice(w, 2 * w), *b_in, y_ref, *b_scr)]
        turns = [live[0], live[1], live[0], live[2]]
        while live:
            for gen in turns:
                if gen in live and next(gen, "done") == "done":
                    live.remove(gen)
        return carry

    lax.fori_loop(0, chunks, chunk, 0)


def _mixers(proj, a_par, b_par, c_par, bsz, t_pad, tb):
    m = proj.shape[0]
    nt = t_pad // tb
    w = BRANCH_WIDTH
    qw = B_HEADS * B_DQK

    def col(off, width):
        return pl.BlockSpec((tb, width), lambda b, t: (b * nt + t, off // width))

    def whole(a):
        return pl.BlockSpec(a.shape, lambda b, t: (0,) * a.ndim)

    a_cols = [col(A_Q, w), col(A_F, w), col(A_I, w), col(A_Z, w)]
    b_cols = [col(B_Q, qw), col(B_K, qw), col(B_V, w), col(B_O, w), col(B_Z, w), col(B_G, LANES)]
    c_cols = [col(C_R, w), col(C_K, w), col(C_V, w), col(C_Z, w), col(C_LR, LANES)]
    conv_w = b_par[0]
    b_specs = [pl.BlockSpec((B_CONV, qw), lambda b, t: (0, 0)),
               pl.BlockSpec((B_CONV, qw), lambda b, t: (0, 1))] + [whole(a) for a in b_par[1:]]
    operands = ([proj] * len(a_cols) + list(a_par) + [proj] * len(b_cols) + [conv_w] + list(b_par)
                + [proj] * len(c_cols) + list(c_par))
    in_specs = (a_cols + [whole(a) for a in a_par] + b_cols + b_specs + c_cols + [whole(a) for a in c_par])
    assert len(a_cols) + len(a_par) == N_A_IN and len(b_cols) + len(b_specs) == N_B_IN
    assert len(c_cols) + len(c_par) == N_C_IN
    return pl.pallas_call(
        functools.partial(_mixers_kernel, chunks=tb // CHUNK),
        grid=(bsz, nt),
        in_specs=in_specs,
        out_specs=pl.BlockSpec((tb, N_BRANCH * w), lambda b, t: (b * nt + t, 0)),
        out_shape=jax.ShapeDtypeStruct((m, N_BRANCH * w), BF16),
        scratch_shapes=[pltpu.VMEM((A_HEADS, LANES, LANES), F32),
                        pltpu.VMEM((B_HEADS, B_DQK, B_DV + LANES), F32),
                        pltpu.VMEM((B_HEADS, SUBLANES, LANES), F32),
                        pltpu.VMEM((SUBLANES, qw), F32),
                        pltpu.VMEM((SUBLANES, qw), F32),
                        pltpu.VMEM((w // LANES, LANES, LANES), F32),
                        pltpu.VMEM((SUBLANES, w), F32),
                        pltpu.VMEM((SUBLANES, w), F32),
                        pltpu.VMEM((SUBLANES, w), F32),
                        pltpu.VMEM((SUBLANES, LANES), F32)],
        compiler_params=_params(("arbitrary", "arbitrary")),
        name="mixers",
    )(*operands)


def _post_kernel(y_ref, g_ref, wbr_ref, wout_ref, h_ref, gn_ref,
                 o_ref, hg_ref, rs_ref, mg_ref, ss_ref, *, nsub, tn):
    j = pl.program_id(2)

    @pl.when(j == 0)
    def _():
        ss_ref[...] = jnp.zeros_like(ss_ref)

    @pl.when(j < nsub)
    def _():
        w = BRANCH_WIDTH
        acc = _sigmoid(g_ref[:, :tn]) * _dot(y_ref[:, :w], wbr_ref[0])
        for b in range(1, N_BRANCH):
            acc = acc + _sigmoid(g_ref[:, b * tn:(b + 1) * tn]) * _dot(y_ref[:, b * w:(b + 1) * w], wbr_ref[b])
        mg_ref[j] = acc.astype(BF16)

    @pl.when(j >= nsub)
    def _():
        new = h_ref[...]
        for kk in range(nsub):
            new = new + _dot(mg_ref[kk], wout_ref[kk * tn:(kk + 1) * tn, :])
        row = lax.broadcasted_iota(jnp.int32, new.shape, 0)
        keep = jnp.logical_or(pl.program_id(1) > 0, row >= PAD_FRONT)
        new = jnp.where(keep, new, 0.0)
        o_ref[...] = new
        hg_ref[...] = (new * gn_ref[...]).astype(hg_ref.dtype)
        ss_ref[...] += jnp.sum(new * new, axis=-1, keepdims=True)

    @pl.when(j == 2 * nsub - 1)
    def _():
        rs_ref[...] = lax.rsqrt(ss_ref[...] * (1.0 / (nsub * tn)) + NORM_EPS)


def _post(y, proj, w_br, w_out, h, g_next, layer, bsz, t_pad, tm, tn):
    m, d = h.shape
    nt = t_pad // tm
    nsub = d // tn
    assert tn == GATE_TILE and d == D_GATE
    y_spec = pl.BlockSpec((tm, N_BRANCH * BRANCH_WIDTH), lambda b, t, j: (b * nt + t, 0))
    gate_spec = pl.BlockSpec((tm, N_BRANCH * tn),
                             lambda b, t, j: (b * nt + t, G_BASE // (N_BRANCH * tn) + jnp.minimum(j, nsub - 1)))

    def out_col(j):
        return jnp.maximum(j - nsub, 0)

    return pl.pallas_call(
        functools.partial(_post_kernel, nsub=nsub, tn=tn),
        grid=(bsz, nt, 2 * nsub),
        in_specs=[y_spec, gate_spec,
                  pl.BlockSpec((None, 3, BRANCH_WIDTH, tn), lambda b, t, j: (layer, 0, 0, jnp.minimum(j, nsub - 1))),
                  pl.BlockSpec((None, d, tn), lambda b, t, j: (layer, 0, out_col(j))),
                  pl.BlockSpec((tm, tn), lambda b, t, j: (b * nt + t, out_col(j))),
                  pl.BlockSpec((1, tn), lambda b, t, j: (0, out_col(j)))],
        out_specs=[pl.BlockSpec((tm, tn), lambda b, t, j: (b * nt + t, out_col(j))),
                   pl.BlockSpec((tm, tn), lambda b, t, j: (b * nt + t, out_col(j))),
                   pl.BlockSpec((tm, LANES), lambda b, t, j: (b * nt + t, 0))],
        out_shape=[jax.ShapeDtypeStruct((m, d), F32), jax.ShapeDtypeStruct((m, d), BF16),
                   jax.ShapeDtypeStruct((m, LANES), F32)],
        scratch_shapes=[pltpu.VMEM((nsub, tm, tn), BF16), pltpu.VMEM((tm, LANES), F32)],
        input_output_aliases={4: 0},
        compiler_params=_params(("arbitrary", "arbitrary", "arbitrary")),
        name="post",
    )(y, proj, w_br, w_out, h, g_next)


SRC_B_G = B_O + BRANCH_WIDTH
SRC_C_LR = SRC_B_G + 2 * B_HEADS + 4 * BRANCH_WIDTH
N_IN = SRC_C_LR + LANES + BRANCH_WIDTH + N_BRANCH * D_GATE
RG_TILE = 512
RG_SRC_TILES = -(-N_IN // RG_TILE)
RG_ROWS = 256
RG_RUNS = ((0, SRC_B_G // RG_TILE, 0),
           (B_Z // RG_TILE, C_Z // RG_TILE, SRC_B_G + 2 * B_HEADS - B_Z),
           (C_Z // RG_TILE, B_G // RG_TILE, SRC_C_LR + LANES - C_Z))


def _regroup_kernel(x_ref, o_ref, t2_ref, g_ref, c_ref):
    s = pl.program_id(1)
    d = t2_ref.shape[0]

    @pl.when(s >= 1)
    def _():
        t2_ref[:, :RG_TILE] = t2_ref[:, RG_TILE:]

    @pl.when(s < RG_SRC_TILES)
    def _():
        for r in range(RG_TILE // LANES):
            t2_ref[:, RG_TILE + r * LANES:RG_TILE + (r + 1) * LANES] = x_ref[r * LANES:(r + 1) * LANES, :].T

    @pl.when(s == SRC_B_G // RG_TILE)
    def _():
        g_ref[...] = t2_ref[:, RG_TILE:RG_TILE + LANES]

    @pl.when(s == SRC_C_LR // RG_TILE)
    def _():
        c_ref[...] = t2_ref[:, RG_TILE:RG_TILE + 2 * LANES]

    def emit(shift):
        base = shift // LANES * LANES
        for r in range(0, d, RG_ROWS):
            if shift == base:
                tile = t2_ref[r:r + RG_ROWS, base:base + RG_TILE]
            else:
                tile = t2_ref[r:r + RG_ROWS, base:base + RG_TILE + LANES][:, shift - base:shift - base + RG_TILE]
            o_ref[r:r + RG_ROWS, :] = tile.astype(BF16)

    t = s - 1
    for first, end, shift in RG_RUNS:
        pl.when(jnp.logical_and(t >= first, t < end))(functools.partial(emit, shift))

    @pl.when(t == B_G // RG_TILE)
    def _():
        lane = lax.broadcasted_iota(jnp.int32, (RG_ROWS, LANES), 1)
        off = SRC_C_LR % RG_TILE
        for r in range(0, d, RG_ROWS):
            rows = slice(r, r + RG_ROWS)
            o_ref[rows, :LANES] = jnp.where(lane < 2 * B_HEADS, g_ref[rows, :], 0.0).astype(BF16)
            o_ref[rows, LANES:2 * LANES] = c_ref[rows, :][:, off:off + LANES].astype(BF16)
            o_ref[rows, 2 * LANES:] = jnp.zeros((RG_ROWS, RG_TILE - 2 * LANES), BF16)


def _gate_tile_position(t):
    first = G_BASE // RG_TILE
    per_branch = D_GATE // RG_TILE
    g = t - first
    moved = first + N_BRANCH * (g % per_branch) + g // per_branch
    return jnp.where(jnp.logical_and(g >= 0, g < N_BRANCH * per_branch), moved, t)


def _regroup_w_in(w_in):
    depth, d, n_in = w_in.shape
    assert n_in == N_IN and d % RG_ROWS == 0 and N_PROJ == (B_G // RG_TILE + 1) * RG_TILE
    assert C_LR == B_G + LANES and SRC_C_LR % RG_TILE + LANES <= 2 * LANES and RG_TILE == GATE_TILE
    return pl.pallas_call(
        _regroup_kernel,
        grid=(depth, RG_SRC_TILES + 1),
        in_specs=[pl.BlockSpec((None, RG_TILE, d), lambda l, s: (l, jnp.minimum(s, RG_SRC_TILES - 1), 0))],
        out_specs=pl.BlockSpec((None, d, RG_TILE), lambda l, s: (l, 0, _gate_tile_position(jnp.maximum(s - 1, 0)))),
        out_shape=jax.ShapeDtypeStruct((depth, d, N_PROJ), BF16),
        scratch_shapes=[pltpu.VMEM((d, 2 * RG_TILE), F32), pltpu.VMEM((d, LANES), F32),
                        pltpu.VMEM((d, 2 * LANES), F32)],
        compiler_params=_params(("arbitrary", "arbitrary")),
        name="regroup_w_in",
    )(jnp.swapaxes(w_in, 1, 2))


def _rwkv_up_weights(w_up, a_up):
    top = jnp.concatenate([w_up, jnp.zeros_like(w_up)], axis=-1)
    bot = jnp.concatenate([jnp.zeros_like(a_up), a_up], axis=-1)
    w2 = jnp.concatenate([top, bot], axis=1)
    hi = w2.astype(BF16)
    lo = (w2 - hi.astype(F32)).astype(BF16)
    return hi, lo


def kernel(x, meta_tokens, norm_g, w_in, hgrn_lb_logits, hgrn_norm_g, mlstm_conv, mlstm_ig_b, mlstm_fg_b,
           mlstm_norm_g, rwkv_mu, rwkv_w0, rwkv_w_up, rwkv_a0, rwkv_a_up, rwkv_k_k, rwkv_k_a, rwkv_r_k,
           rwkv_ln_g, rwkv_ln_b, w_br, w_out, final_norm_g):
    bsz, seq, d = x.shape
    depth = w_in.shape[0]
    t_pad = PAD_FRONT + N_META + seq
    m = bsz * t_pad
    assert seq % (2 * CHUNK) == 0 and t_pad % CHUNK == 0

    tb = _divisor_tile(t_pad, 320, CHUNK)
    tm_in = _divisor_tile(m, 3328, LANES)
    tm_row = _divisor_tile(m, 640, LANES)
    tm_post = _divisor_tile(t_pad, 1040, 2 * SUBLANES)
    assert tm_post >= PAD_FRONT

    w_in_p = _regroup_w_in(w_in)
    w_br_b = w_br.astype(BF16)
    w_out_b = w_out.astype(BF16)
    p = jax.nn.softmax(hgrn_lb_logits.astype(F32), axis=0)
    lower_bounds = jnp.cumsum(p, axis=0) - p[0]
    gate_b = jnp.concatenate([mlstm_ig_b, mlstm_fg_b, jnp.zeros((depth, LANES - 2 * B_HEADS), F32)], axis=-1)
    w2h, w2l = _rwkv_up_weights(rwkv_w_up, rwkv_a_up)
    w = BRANCH_WIDTH
    mu_r, mu_k, mu_v, mu_lr = rwkv_mu[:, :w], rwkv_mu[:, w:2 * w], rwkv_mu[:, 2 * w:3 * w], rwkv_mu[:, 3 * w:]
    tri = jnp.asarray(np.tril(np.ones((CHUNK, CHUNK), np.float32)), BF16)
    levels = jnp.asarray(_hgrn2_levels())
    lane_head = np.arange(LANES) // C_DH
    ones_bd = jnp.asarray((lane_head[:, None] == lane_head[None, :]).astype(np.float32), BF16)

    meta = jnp.broadcast_to(meta_tokens[None].astype(F32), (bsz, N_META, d))
    h = jnp.concatenate([jnp.zeros((bsz, PAD_FRONT, d), F32), meta, x.astype(F32)], axis=1).reshape(m, d)

    hg = _rmsnorm(h, norm_g[0][None, :], BF16, tm_row)
    rs = jnp.ones((m, LANES), F32)
    for l in range(depth):
        row = lambda a: a[l][None, :]
        proj = _inproj(hg, rs, w_in_p, l, tm_in, 512)
        a_par = (row(lower_bounds), row(hgrn_norm_g), tri, levels)
        b_par = (mlstm_conv[l], row(gate_b), row(mlstm_norm_g), tri)
        c_par = (row(mu_r), row(mu_k), row(mu_v), row(mu_lr), row(rwkv_w0), row(rwkv_a0), row(rwkv_k_k),
                 row(rwkv_k_a), row(rwkv_r_k), row(rwkv_ln_g), row(rwkv_ln_b), w2h[l], w2l[l], tri, ones_bd)
        y = _mixers(proj, a_par, b_par, c_par, bsz, t_pad, tb)
        g_next = norm_g[min(l + 1, depth - 1)][None, :]
        h, hg, rs = _post(y, proj, w_br_b, w_out_b, h, g_next, l, bsz, t_pad, tm_post, GATE_TILE)

    return _final_norm(h, final_norm_g[None, :], bsz, t_pad, seq)
```
